```python
import jax, jax.numpy as jnp
from jax import lax
import numpy as np

D_MODEL = 2048
BATCH = 2
SEQ = 16384
DEPTH = 4

CTX_LEN = 256
GRID_W = 64
HEAD_DIM = 128
N_HEADS = (D_MODEL // 2) // HEAD_DIM
N_KV_HEADS = N_HEADS // 4
GQA_GROUP = N_HEADS // N_KV_HEADS
Q_WIDTH = N_HEADS * HEAD_DIM
KV_WIDTH = N_KV_HEADS * HEAD_DIM
N_FOURIER_GROUPS = 4
FOURIER_WIDTH = D_MODEL // 2
FOURIER_GROUP_DIM = FOURIER_WIDTH // N_FOURIER_GROUPS
EVEN_IN_WIDTH = FOURIER_WIDTH + Q_WIDTH + 2 * KV_WIDTH
EVEN_MIX_WIDTH = FOURIER_WIDTH + Q_WIDTH
WINDOW = 128
BLOCK = 128
ATTN_SCALE = HEAD_DIM ** -0.5
ROPE_THETA = 10000.0
ROPE_AXIS_DIM = HEAD_DIM // 2
CONV_WIDTH = 3
CONV_DIM = D_MODEL
D_FF = 4 * D_MODEL
N_MOD = 6
NORM_EPS = 1e-6
NEG_INF = -1e30

kernel_name = "hybrid_fourier_swa_shortconv_dit"


def rmsnorm(t, gain):
    tf = t.astype(jnp.float32)
    tf = tf * lax.rsqrt(jnp.mean(tf * tf, axis=-1, keepdims=True) + NORM_EPS)
    return tf.astype(t.dtype) * gain


def modulate(t, gain, shift, scale):
    return rmsnorm(t, gain) * (1 + scale) + shift


def axial_rope_tables(n_tok):
    rows = n_tok // GRID_W
    row = jnp.repeat(jnp.arange(rows, dtype=jnp.float32), GRID_W)
    col = jnp.tile(jnp.arange(GRID_W, dtype=jnp.float32), rows)
    inv_freq = ROPE_THETA ** (-jnp.arange(0, ROPE_AXIS_DIM, 2, dtype=jnp.float32) / ROPE_AXIS_DIM)
    ang = jnp.stack([row[:, None] * inv_freq, col[:, None] * inv_freq], axis=1)
    return jnp.cos(ang), jnp.sin(ang)


def apply_axial_rope(t, cos, sin):
    b, n, h, _ = t.shape
    tf = t.astype(jnp.float32).reshape(b, n, h, 2, 2, ROPE_AXIS_DIM // 2)
    t1, t2 = tf[..., 0, :], tf[..., 1, :]
    cs, sn = cos[None, :, None], sin[None, :, None]
    out = jnp.stack([t1 * cs - t2 * sn, t2 * cs + t1 * sn], axis=-2)
    return out.reshape(b, n, h, HEAD_DIM).astype(t.dtype)


def to_heads(t, n_heads):
    return t.reshape(t.shape[0], t.shape[1], n_heads, HEAD_DIM)


def fourier_mix(u):
    b, n, _ = u.shape
    ug = u.astype(jnp.float32).reshape(b, n, N_FOURIER_GROUPS, FOURIER_GROUP_DIM)
    y = jnp.fft.fft2(ug, axes=(1, 3)).real * (n * FOURIER_GROUP_DIM) ** -0.5
    return y.reshape(b, n, FOURIER_WIDTH).astype(u.dtype)


def sink_logits(sink):
    return sink.astype(jnp.float32).reshape(N_KV_HEADS, GQA_GROUP, 1, 1)


def sink_softmax(scores, sink_l):
    m = sink_l
    for s in scores:
        m = jnp.maximum(m, s.max(axis=-1, keepdims=True))
    exps = [jnp.exp(s - m) for s in scores]
    den = jnp.exp(sink_l - m)
    for e in exps:
        den = den + e.sum(axis=-1, keepdims=True)
    return [e / den for e in exps]


def band_blocks(t):
    b, n = t.shape[:2]
    tb = t.reshape(b, n // BLOCK, BLOCK, N_KV_HEADS, HEAD_DIM)
    tp = jnp.pad(tb, ((0, 0), (1, 1), (0, 0), (0, 0), (0, 0)))
    return jnp.concatenate([tp[:, :-2], tp[:, 1:-1], tp[:, 2:]], axis=2)


def window_attention(q, k, v, k_ctx, v_ctx, sink):
    b, n = q.shape[:2]
    nb = n // BLOCK
    qb = q.reshape(b, nb, BLOCK, N_KV_HEADS, GQA_GROUP, HEAD_DIM)
    kb, vb = band_blocks(k), band_blocks(v)
    s_loc = jnp.einsum('bnqkgd,bnskd->bnkgqs', qb, kb, preferred_element_type=jnp.float32) * ATTN_SCALE
    s_ctx = jnp.einsum('bnqkgd,bckd->bnkgqc', qb, k_ctx, preferred_element_type=jnp.float32) * ATTN_SCALE
    qi = jnp.arange(BLOCK)[:, None]
    kj = jnp.arange(3 * BLOCK)[None, :]
    kpos = (jnp.arange(nb)[:, None, None] - 1) * BLOCK + kj
    valid = (jnp.abs(kj - BLOCK - qi) <= WINDOW) & (kpos >= 0) & (kpos < n)
    s_loc = jnp.where(valid[None, :, None, None], s_loc, NEG_INF)
    p_loc, p_ctx = sink_softmax([s_loc, s_ctx], sink_logits(sink))
    o = (jnp.einsum('bnkgqs,bnskd->bnqkgd', p_loc.astype(v.dtype), vb)
         + jnp.einsum('bnkgqc,bckd->bnqkgd', p_ctx.astype(v.dtype), v_ctx))
    return o.reshape(b, n, Q_WIDTH)


def context_attention(q_c, k_c, v_c, sink):
    b, m = q_c.shape[:2]
    qg = q_c.reshape(b, m, N_KV_HEADS, GQA_GROUP, HEAD_DIM)
    s = jnp.einsum('bqkgd,bckd->bkgqc', qg, k_c, preferred_element_type=jnp.float32) * ATTN_SCALE
    (p,) = sink_softmax([s], sink_logits(sink))
    o = jnp.einsum('bkgqc,bckd->bqkgd', p.astype(v_c.dtype), v_c)
    return o.reshape(b, m, Q_WIDTH)


def split_even(p):
    return jnp.split(p, [FOURIER_WIDTH, FOURIER_WIDTH + Q_WIDTH, FOURIER_WIDTH + Q_WIDTH + KV_WIDTH], axis=-1)


def short_conv_mix(u, w_in, conv_w, w_out):
    gb, gc, xv = jnp.split(u @ w_in, 3, axis=-1)
    z = gc * xv
    zp = jnp.pad(z, ((0, 0), (1, 1), (0, 0)))
    y = conv_w[0] * zp[:, :-2] + conv_w[1] * zp[:, 1:-1] + conv_w[2] * zp[:, 2:]
    return (gb * y) @ w_out


def sq_relu_mlp(u, w1, w2):
    return jnp.square(jax.nn.relu(u @ w1)) @ w2


def setup_inputs(seed: int = 0) -> dict:
    key = jax.random.key(seed)
    ks = jax.random.split(key, 16)
    n_att = (DEPTH + 1) // 2
    n_conv = DEPTH // 2

    def nrm(k, shape, fan_in):
        return jax.random.normal(k, shape, jnp.float32) * fan_in ** -0.5

    return {
        "x": jax.random.normal(ks[0], (BATCH, SEQ, D_MODEL), jnp.float32),
        "c": jax.random.normal(ks[1], (BATCH, D_MODEL), jnp.float32),
        "ctx": jax.random.normal(ks[2], (BATCH, CTX_LEN, D_MODEL), jnp.float32),
        "c_ctx": jax.random.normal(ks[3], (D_MODEL,), jnp.float32),
        "w_mod": nrm(ks[4], (DEPTH, D_MODEL, N_MOD * D_MODEL), D_MODEL),
        "b_mod": 0.01 * jax.random.normal(ks[5], (DEPTH, N_MOD * D_MODEL), jnp.float32),
        "norm_gains": 1.0 + 0.01 * jax.random.normal(ks[6], (DEPTH, 4, D_MODEL), jnp.float32),
        "att_w_in": nrm(ks[7], (n_att, D_MODEL, EVEN_IN_WIDTH), D_MODEL),
        "att_sink": jax.random.normal(ks[8], (n_att, N_HEADS), jnp.float32),
        "att_w_out": nrm(ks[9], (n_att, EVEN_MIX_WIDTH, D_MODEL), EVEN_MIX_WIDTH),
        "conv_w_in": nrm(ks[10], (n_conv, D_MODEL, 3 * CONV_DIM), D_MODEL),
        "conv_w": nrm(ks[11], (n_conv, CONV_WIDTH, CONV_DIM), CONV_WIDTH),
        "conv_w_out": nrm(ks[12], (n_conv, CONV_DIM, D_MODEL), CONV_DIM),
        "mlp_w1": nrm(ks[13], (DEPTH, D_MODEL, D_FF), D_MODEL),
        "mlp_w2": nrm(ks[14], (DEPTH, D_FF, D_MODEL), D_FF),
    }


def reference(x, c, ctx, c_ctx, w_mod, b_mod, norm_gains, att_w_in, att_sink, att_w_out,
              conv_w_in, conv_w, conv_w_out, mlp_w1, mlp_w2):
    n_tok = x.shape[1]
    cos, sin = axial_rope_tables(n_tok)
    silu_c = jax.nn.silu(c)
    silu_cc = jax.nn.silu(c_ctx)
    h_ctx = ctx
    for l in range(DEPTH):
        g = norm_gains[l]
        is_attn = l % 2 == 0
        ctx_update = any(j % 2 == 0 for j in range(l + 1, DEPTH))
        sh_m, sc_m, gt_m, sh_f, sc_f, gt_f = [t[:, None] for t in
                                              jnp.split(silu_c @ w_mod[l] + b_mod[l], N_MOD, axis=-1)]
        if is_attn or ctx_update:
            csh_m, csc_m, cgt_m, csh_f, csc_f, cgt_f = jnp.split(silu_cc @ w_mod[l] + b_mod[l], N_MOD, axis=-1)

        ux = modulate(x, g[0], sh_m, sc_m)
        if is_attn:
            a = l // 2
            w_in, w_out, sink = att_w_in[a], att_w_out[a], att_sink[a]
            uc = modulate(h_ctx, g[0], csh_m, csc_m)
            if ctx_update:
                f_c, q_c, k_c, v_c = split_even(uc @ w_in)
            else:
                k_c, v_c = jnp.split(uc @ w_in[:, FOURIER_WIDTH + Q_WIDTH:], 2, axis=-1)
            k_c, v_c = to_heads(k_c, N_KV_HEADS), to_heads(v_c, N_KV_HEADS)
            f_x, q_x, k_x, v_x = split_even(ux @ w_in)
            q_x = apply_axial_rope(to_heads(q_x, N_HEADS), cos, sin)
            k_x = apply_axial_rope(to_heads(k_x, N_KV_HEADS), cos, sin)
            v_x = to_heads(v_x, N_KV_HEADS)
            attn_x = window_attention(q_x, k_x, v_x, k_c, v_c, sink)
            mix_x = jnp.concatenate([fourier_mix(f_x), attn_x], axis=-1) @ w_out
            if ctx_update:
                attn_c = context_attention(to_heads(q_c, N_HEADS), k_c, v_c, sink)
                mix_c = jnp.concatenate([fourier_mix(f_c), attn_c], axis=-1) @ w_out
        else:
            m = l // 2
            mix_x = short_conv_mix(ux, conv_w_in[m], conv_w[m], conv_w_out[m])
            if ctx_update:
                mix_c = short_conv_mix(modulate(h_ctx, g[0], csh_m, csc_m), conv_w_in[m], conv_w[m], conv_w_out[m])

        x = x + gt_m * rmsnorm(mix_x, g[1])
        x = x + gt_f * rmsnorm(sq_relu_mlp(modulate(x, g[2], sh_f, sc_f), mlp_w1[l], mlp_w2[l]), g[3])
        if ctx_update:
            h_ctx = h_ctx + cgt_m * rmsnorm(mix_c, g[1])
            h_ctx = h_ctx + cgt_f * rmsnorm(sq_relu_mlp(modulate(h_ctx, g[2], csh_f, csc_f), mlp_w1[l], mlp_w2[l]), g[3])
    return x
```

```python
import functools

import numpy as np
import jax
import jax.numpy as jnp
from jax import lax
from jax.experimental import pallas as pl
from jax.experimental.pallas import tpu as pltpu

F32 = jnp.float32
BF16 = jnp.bfloat16

D_MODEL = 2048
HEAD_DIM = 128
N_HEADS = 8
N_KV_HEADS = 2
GQA_GROUP = 4
Q_WIDTH = 1024
KV_WIDTH = 256
FOURIER_WIDTH = 1024
N_FOURIER_GROUPS = 4
FOURIER_GROUP_DIM = 256
EVEN_IN_WIDTH = 2560
BLOCK = 128
GRID_W = 64
ROPE_THETA = 10000.0
ROPE_AXIS_DIM = 64
ATTN_SCALE = HEAD_DIM ** -0.5
D_FF = 8192
N_MOD = 6
NORM_EPS = 1e-6
NEG_INF = -1e30
DEPTH = 4

MOD_ROWS = 8
DFT_COLS = 128
CONV_HALO = 16
VMEM_LIMIT = 56 * 1024 * 1024


def _params(sem):
    return pltpu.CompilerParams(dimension_semantics=sem, vmem_limit_bytes=VMEM_LIMIT)


def _rms(t):
    return t * lax.rsqrt(jnp.mean(t * t, axis=-1, keepdims=True) + NORM_EPS)


def _modulated(x, gain, shift, scale):
    return (_rms(x) * (gain * (1.0 + scale)) + shift).astype(BF16)


def _mod_kernel(c_ref, w_ref, b_ref, o_ref):
    cf = c_ref[...]
    s = (cf * jax.nn.sigmoid(cf)).astype(BF16)
    o_ref[0] = jnp.dot(s, w_ref[0].astype(BF16), preferred_element_type=F32) + b_ref[0]


def _modulation(cc, w_mod, b_mod):
    depth, d, n = w_mod.shape
    tn = 1024
    return pl.pallas_call(
        _mod_kernel,
        grid=(depth, n // tn),
        in_specs=[pl.BlockSpec((MOD_ROWS, d), lambda l, j: (0, 0)),
                  pl.BlockSpec((1, d, tn), lambda l, j: (l, 0, j)),
                  pl.BlockSpec((1, 1, tn), lambda l, j: (l, 0, j))],
        out_specs=pl.BlockSpec((1, MOD_ROWS, tn), lambda l, j: (l, 0, j)),
        out_shape=jax.ShapeDtypeStruct((depth, MOD_ROWS, n), F32),
        compiler_params=_params(("arbitrary", "arbitrary")),
        name="modulation",
    )(cc, w_mod, b_mod.reshape(depth, 1, n))


def _even_in_kernel(x_ref, mod_ref, g_ref, w_ref, cs_ref, cos_ref, sin_ref,
                    u_ref, q_ref, k_ref, v_ref):
    tm = x_ref.shape[1]
    u = _modulated(x_ref[0], g_ref[0:1, :], mod_ref[0, 0:1, :], mod_ref[0, 1:2, :])
    p = jnp.dot(u, w_ref[...], preferred_element_type=F32)
    cs = cs_ref[...]
    gd = FOURIER_GROUP_DIM
    for g in range(N_FOURIER_GROUPS):
        t = jnp.dot(p[:, g * gd:(g + 1) * gd].astype(BF16), cs, preferred_element_type=F32)
        u_ref[0, :, g * gd:(g + 1) * gd] = t[:, :gd].astype(BF16)
        u_ref[0, :, FOURIER_WIDTH + g * gd:FOURIER_WIDTH + (g + 1) * gd] = t[:, gd:].astype(BF16)
    cos = cos_ref[...]
    sin = sin_ref[...]
    lane = lax.broadcasted_iota(jnp.int32, (tm, HEAD_DIM), 1)
    first_half = (lane % ROPE_AXIS_DIM) < (ROPE_AXIS_DIM // 2)

    def rope(h):
        partner = jnp.where(first_half, pltpu.roll(h, HEAD_DIM - 32, 1), pltpu.roll(h, 32, 1))
        return h * cos + partner * sin

    q0 = FOURIER_WIDTH
    for h in range(N_HEADS):
        hq = p[:, q0 + h * HEAD_DIM:q0 + (h + 1) * HEAD_DIM]
        q_ref[0, :, h * HEAD_DIM:(h + 1) * HEAD_DIM] = (rope(hq) * ATTN_SCALE).astype(BF16)
    k0 = FOURIER_WIDTH + Q_WIDTH
    for h in range(N_KV_HEADS):
        hk = p[:, k0 + h * HEAD_DIM:k0 + (h + 1) * HEAD_DIM]
        k_ref[0, :, h * HEAD_DIM:(h + 1) * HEAD_DIM] = rope(hk).astype(BF16)
    v_ref[0] = p[:, k0 + KV_WIDTH:].astype(BF16)


def _even_in(x, mod, gains, w_in, cs, cos, sin, tm):
    b, n, d = x.shape
    tok = lambda bi, i: (bi, i, 0)
    const = lambda bi, i: (0, 0)
    return pl.pallas_call(
        _even_in_kernel,
        grid=(b, n // tm),
        in_specs=[pl.BlockSpec((1, tm, d), tok),
                  pl.BlockSpec((1, MOD_ROWS, d), lambda bi, i: (bi, 0, 0)),
                  pl.BlockSpec((MOD_ROWS, d), const),
                  pl.BlockSpec((d, EVEN_IN_WIDTH), const),
                  pl.BlockSpec((FOURIER_GROUP_DIM, 2 * FOURIER_GROUP_DIM), const),
                  pl.BlockSpec((tm, HEAD_DIM), lambda bi, i: (i, 0)),
                  pl.BlockSpec((tm, HEAD_DIM), lambda bi, i: (i, 0))],
        out_specs=[pl.BlockSpec((1, tm, 2 * FOURIER_WIDTH), tok),
                   pl.BlockSpec((1, tm, Q_WIDTH), tok),
                   pl.BlockSpec((1, tm, KV_WIDTH), tok),
                   pl.BlockSpec((1, tm, KV_WIDTH), tok)],
        out_shape=[jax.ShapeDtypeStruct((b, n, 2 * FOURIER_WIDTH), BF16),
                   jax.ShapeDtypeStruct((b, n, Q_WIDTH), BF16),
                   jax.ShapeDtypeStruct((b, n, KV_WIDTH), BF16),
                   jax.ShapeDtypeStruct((b, n, KV_WIDTH), BF16)],
        compiler_params=_params(("arbitrary", "arbitrary")),
        name="even_in_proj",
    )(x, mod, gains, w_in, cs, cos, sin)


def _dft1_kernel(u_ref, t_ref, o_ref):
    for j in range(u_ref.shape[1]):
        xs = jnp.concatenate([u_ref[0, j, :, :FOURIER_WIDTH], u_ref[0, j, :, FOURIER_WIDTH:]], axis=0)
        o_ref[0, j] = jnp.dot(t_ref[j], xs, preferred_element_type=F32).astype(BF16)


def _dft1(ut, tab, tb):
    b, nb, na, w = ut.shape
    return pl.pallas_call(
        _dft1_kernel,
        grid=(nb // tb, b),
        in_specs=[pl.BlockSpec((1, tb, na, w), lambda j, bi: (bi, j, 0, 0)),
                  pl.BlockSpec((tb, 2 * na, 2 * na), lambda j, bi: (j, 0, 0))],
        out_specs=pl.BlockSpec((1, tb, 2 * na, FOURIER_WIDTH), lambda j, bi: (bi, j, 0, 0)),
        out_shape=jax.ShapeDtypeStruct((b, nb, 2 * na, FOURIER_WIDTH), BF16),
        compiler_params=_params(("arbitrary", "arbitrary")),
        name="seq_dft_stage1",
    )(ut, tab)


def _dft2_kernel(a_ref, m_ref, o_ref):
    for j in range(a_ref.shape[1]):
        o_ref[0, j] = jnp.dot(m_ref[...], a_ref[0, j], preferred_element_type=F32).astype(BF16)


def _dft2(at, m2, td):
    b, na, k2, w = at.shape
    nb = k2 // 2
    return pl.pallas_call(
        _dft2_kernel,
        grid=(b, na // td),
        in_specs=[pl.BlockSpec((1, td, k2, w), lambda bi, j: (bi, j, 0, 0)),
                  pl.BlockSpec((nb, k2), lambda bi, j: (0, 0))],
        out_specs=pl.BlockSpec((1, td, nb, w), lambda bi, j: (bi, j, 0, 0)),
        out_shape=jax.ShapeDtypeStruct((b, na, nb, w), BF16),
        compiler_params=_params(("arbitrary", "arbitrary")),
        name="seq_dft_stage2",
    )(at, m2)


def _seq_dft_tables(n):
    nb = DFT_COLS
    na = n // nb
    a = np.arange(na)
    dd = np.arange(na)
    bb = np.arange(nb)
    ph = (dd[None, :, None] * (nb * a[None, None, :] + bb[:, None, None])) % n
    ang = 2.0 * np.pi * ph / n
    c1, s1 = np.cos(ang), np.sin(ang)
    tab1 = np.concatenate([np.concatenate([c1, -s1], axis=2),
                           np.concatenate([s1, c1], axis=2)], axis=1)
    cidx = np.arange(nb)
    ang2 = 2.0 * np.pi * ((cidx[:, None] * bb[None, :]) % nb) / nb
    m2 = np.concatenate([np.cos(ang2), -np.sin(ang2)], axis=1) * n ** -0.5
    return jnp.asarray(tab1, BF16), jnp.asarray(m2, BF16)


def _channel_dft_table():
    i = np.arange(FOURIER_GROUP_DIM)
    ang = 2.0 * np.pi * ((i[:, None] * i[None, :]) % FOURIER_GROUP_DIM) / FOURIER_GROUP_DIM
    cs = np.concatenate([np.cos(ang), np.sin(ang)], axis=1) * FOURIER_GROUP_DIM ** -0.5
    return jnp.asarray(cs, BF16)


def _fourier_seq(u):
    b, n, w = u.shape
    nb = DFT_COLS
    na = n // nb
    tab1, m2 = _seq_dft_tables(n)
    ut = u.reshape(b, na, nb, w).transpose(0, 2, 1, 3)
    s1 = _dft1(ut, tab1, tb=8)
    at = s1.reshape(b, nb, 2, na, FOURIER_WIDTH).transpose(0, 3, 2, 1, 4)
    at = at.reshape(b, na, 2 * nb, FOURIER_WIDTH)
    yt = _dft2(at, m2, td=8)
    return yt.transpose(0, 2, 1, 3).reshape(b, n, FOURIER_WIDTH)


def _ctx_dft_kernel(u_ref, m_ref, o_ref):
    n = u_ref.shape[1]
    y = jnp.dot(m_ref[:, :n], u_ref[0, :, :FOURIER_WIDTH], preferred_element_type=F32)
    y = y + jnp.dot(m_ref[:, n:], u_ref[0, :, FOURIER_WIDTH:], preferred_element_type=F32)
    o_ref[0] = y.astype(BF16)


def _fourier_ctx(u):
    b, n, w = u.shape
    i = np.arange(n)
    ang = 2.0 * np.pi * ((i[:, None] * i[None, :]) % n) / n
    m = jnp.asarray(np.concatenate([np.cos(ang), -np.sin(ang)], axis=1) * n ** -0.5, BF16)
    return pl.pallas_call(
        _ctx_dft_kernel,
        grid=(b,),
        in_specs=[pl.BlockSpec((1, n, w), lambda bi: (bi, 0, 0)),
                  pl.BlockSpec((n, 2 * n), lambda bi: (0, 0))],
        out_specs=pl.BlockSpec((1, n, FOURIER_WIDTH), lambda bi: (bi, 0, 0)),
        out_shape=jax.ShapeDtypeStruct((b, n, FOURIER_WIDTH), BF16),
        compiler_params=_params(("arbitrary",)),
        name="ctx_dft",
    )(u, m)


def _sink_column(sink_ref, kvh, rows):
    parts = [jnp.full((rows, 1), sink_ref[kvh * GQA_GROUP + g], F32) for g in range(GQA_GROUP)]
    return jnp.concatenate(parts, axis=0)


def _softmax_pv(s, sink_col, v):
    m = jnp.maximum(sink_col, jnp.max(s, axis=-1, keepdims=True))
    e = jnp.exp(s - m)
    den = jnp.exp(sink_col - m) + jnp.sum(e, axis=-1, keepdims=True)
    return jnp.dot(e.astype(BF16), v, preferred_element_type=F32) / den


def _window_attn_kernel(sink_ref, q_ref, km_ref, kp_ref, kn_ref, vm_ref, vp_ref, vn_ref,
                        kc_ref, vc_ref, o_ref):
    i = pl.program_id(1)
    kvh = pl.program_id(2)
    n_tiles = pl.num_programs(1)
    blocks = q_ref.shape[1] // BLOCK
    n_ctx = kc_ref.shape[1]
    rows = GQA_GROUP * BLOCK
    sink_col = _sink_column(sink_ref, kvh, BLOCK)
    qi = lax.broadcasted_iota(jnp.int32, (rows, 3 * BLOCK + n_ctx), 0) % BLOCK
    kj = lax.broadcasted_iota(jnp.int32, (rows, 3 * BLOCK + n_ctx), 1)

    def band(has_prev, has_next):
        lo = qi + jnp.where(has_prev, 0, BLOCK)
        hi = qi + jnp.where(has_next, BLOCK, 0)
        return ((kj >= lo) & (kj < BLOCK)) | ((kj >= BLOCK) & (kj <= hi)) | (kj >= 2 * BLOCK)

    for jb in range(blocks):
        sl = slice(jb * BLOCK, (jb + 1) * BLOCK)
        q4 = jnp.concatenate([q_ref[0, sl, g * HEAD_DIM:(g + 1) * HEAD_DIM] for g in range(GQA_GROUP)],
                             axis=0)
        if jb == 0:
            k_prev, v_prev, has_prev = kp_ref[0], vp_ref[0], i > 0
        else:
            psl = slice((jb - 1) * BLOCK, jb * BLOCK)
            k_prev, v_prev, has_prev = km_ref[0, psl], vm_ref[0, psl], True
        if jb == blocks - 1:
            k_next, v_next, has_next = kn_ref[0], vn_ref[0], i < n_tiles - 1
        else:
            nsl = slice((jb + 1) * BLOCK, (jb + 2) * BLOCK)
            k_next, v_next, has_next = km_ref[0, nsl], vm_ref[0, nsl], True
        keys = jnp.concatenate([k_prev, k_next, km_ref[0, sl], kc_ref[0]], axis=0)
        vals = jnp.concatenate([v_prev, v_next, vm_ref[0, sl], vc_ref[0]], axis=0)
        s = lax.dot_general(q4, keys, (((1,), (1,)), ((), ())), preferred_element_type=F32)
        s = jnp.where(band(has_prev, has_next), s, NEG_INF)
        o = _softmax_pv(s, sink_col, vals)
        for g in range(GQA_GROUP):
            o_ref[0, sl, g * HEAD_DIM:(g + 1) * HEAD_DIM] = o[g * BLOCK:(g + 1) * BLOCK].astype(BF16)


def _window_attention(q, k, v, kc, vc, sink, tq):
    b, n, _ = q.shape
    n_ctx = kc.shape[1]
    bpt = tq // BLOCK
    nblk = n // BLOCK
    main = lambda bi, i, h: (bi, i, h)
    prev = lambda bi, i, h: (bi, jnp.maximum(i * bpt - 1, 0), h)
    nxt = lambda bi, i, h: (bi, jnp.minimum((i + 1) * bpt, nblk - 1), h)
    ctx = lambda bi, i, h: (bi, 0, h)
    kv_main = pl.BlockSpec((1, tq, HEAD_DIM), main)
    kv_prev = pl.BlockSpec((1, BLOCK, HEAD_DIM), prev)
    kv_next = pl.BlockSpec((1, BLOCK, HEAD_DIM), nxt)
    kv_ctx = pl.BlockSpec((1, n_ctx, HEAD_DIM), ctx)
    return pl.pallas_call(
        _window_attn_kernel,
        grid=(b, n // tq, N_KV_HEADS),
        in_specs=[pl.BlockSpec(memory_space=pltpu.SMEM),
                  pl.BlockSpec((1, tq, GQA_GROUP * HEAD_DIM), main),
                  kv_main, kv_prev, kv_next, kv_main, kv_prev, kv_next, kv_ctx, kv_ctx],
        out_specs=pl.BlockSpec((1, tq, GQA_GROUP * HEAD_DIM), main),
        out_shape=jax.ShapeDtypeStruct((b, n, Q_WIDTH), BF16),
        compiler_params=_params(("arbitrary", "arbitrary", "arbitrary")),
        name="window_attention",
    )(sink, q, k, k, k, v, v, v, kc, vc)


def _ctx_attn_kernel(sink_ref, q_ref, k_ref, v_ref, o_ref):
    kvh = pl.program_id(1)
    n = q_ref.shape[1]
    q4 = jnp.concatenate([q_ref[0, :, g * HEAD_DIM:(g + 1) * HEAD_DIM] for g in range(GQA_GROUP)], axis=0)
    s = lax.dot_general(q4, k_ref[0], (((1,), (1,)), ((), ())), preferred_element_type=F32)
    o = _softmax_pv(s, _sink_column(sink_ref, kvh, n), v_ref[0])
    for g in range(GQA_GROUP):
        o_ref[0, :, g * HEAD_DIM:(g + 1) * HEAD_DIM] = o[g * n:(g + 1) * n].astype(BF16)


def _ctx_attention(q, k, v, sink):
    b, n, _ = q.shape
    hq = pl.BlockSpec((1, n, GQA_GROUP * HEAD_DIM), lambda bi, h: (bi, 0, h))
    hkv = pl.BlockSpec((1, n, HEAD_DIM), lambda bi, h: (bi, 0, h))
    return pl.pallas_call(
        _ctx_attn_kernel,
        grid=(b, N_KV_HEADS),
        in_specs=[pl.BlockSpec(memory_space=pltpu.SMEM), hq, hkv, hkv],
        out_specs=hq,
        out_shape=jax.ShapeDtypeStruct((b, n, Q_WIDTH), BF16),
        compiler_params=_params(("arbitrary", "arbitrary")),
        name="ctx_attention",
    )(sink, q, k, v)


def _even_out_kernel(f_ref, a_ref, w_ref, x_ref, mod_ref, g_ref, o_ref):
    mix = jnp.dot(f_ref[0], w_ref[:FOURIER_WIDTH, :], preferred_element_type=F32)
    mix = mix + jnp.dot(a_ref[0], w_ref[FOURIER_WIDTH:, :], preferred_element_type=F32)
    o_ref[0] = x_ref[0] + mod_ref[0, 2:3, :] * (_rms(mix) * g_ref[1:2, :])


def _even_out(four, attn, w_out, x, mod, gains, tm):
    b, n, d = x.shape
    tok = lambda bi, i: (bi, i, 0)
    const = lambda bi, i: (0, 0)
    return pl.pallas_call(
        _even_out_kernel,
        grid=(b, n // tm),
        in_specs=[pl.BlockSpec((1, tm, FOURIER_WIDTH), tok),
                  pl.BlockSpec((1, tm, Q_WIDTH), tok),
                  pl.BlockSpec((FOURIER_WIDTH + Q_WIDTH, d), const),
                  pl.BlockSpec((1, tm, d), tok),
                  pl.BlockSpec((1, MOD_ROWS, d), lambda bi, i: (bi, 0, 0)),
                  pl.BlockSpec((MOD_ROWS, d), const)],
        out_specs=pl.BlockSpec((1, tm, d), tok),
        out_shape=jax.ShapeDtypeStruct((b, n, d), F32),
        compiler_params=_params(("arbitrary", "arbitrary")),
        name="even_out_proj",
    )(four, attn, w_out, x, mod, gains)


def _conv_mix_kernel(x_ref, xp_ref, xn_ref, mod_ref, g_ref, wb_ref, wc_ref, wx_ref, cw_ref, wo_ref,
                     o_ref, u_sc, z_sc, acc_sc):
    i = pl.program_id(1)
    j = pl.program_id(2)
    tm = x_ref.shape[1]
    h = CONV_HALO

    @pl.when(j == 0)
    def _():
        gain, shift, scale = g_ref[0:1, :], mod_ref[0, 0:1, :], mod_ref[0, 1:2, :]
        u_sc[0:h, :] = _modulated(xp_ref[0], gain, shift, scale)
        u_sc[h:h + tm, :] = _modulated(x_ref[0], gain, shift, scale)
        u_sc[h + tm:, :] = _modulated(xn_ref[0], gain, shift, scale)

    u_all = u_sc[...]
    gb = jnp.dot(u_sc[h:h + tm, :], wb_ref[...], preferred_element_type=F32)
    z_sc[...] = (jnp.dot(u_all, wc_ref[...], preferred_element_type=F32)
                 * jnp.dot(u_all, wx_ref[...], preferred_element_type=F32))

    @pl.when(i == 0)
    def _():
        z_sc[h - 8:h, :] = jnp.zeros((8, z_sc.shape[1]), F32)

    @pl.when(i == pl.num_programs(1) - 1)
    def _():
        z_sc[h + tm:h + tm + 8, :] = jnp.zeros((8, z_sc.shape[1]), F32)

    y = (cw_ref[0:1, :] * z_sc[h - 1:h - 1 + tm, :]
         + cw_ref[1:2, :] * z_sc[h:h + tm, :]
         + cw_ref[2:3, :] * z_sc[h + 1:h + 1 + tm, :])
    part = jnp.dot((gb * y).astype(BF16), wo_ref[...], preferred_element_type=F32)

    @pl.when(j == 0)
    def _():
        acc_sc[...] = part

    @pl.when(j > 0)
    def _():
        acc_sc[...] += part

    @pl.when(j == pl.num_programs(2) - 1)
    def _():
        o_ref[0] = x_ref[0] + mod_ref[0, 2:3, :] * (_rms(acc_sc[...]) * g_ref[1:2, :])


def _conv_mix(x, mod, gains, w_in, conv_w, w_out, tm, tc):
    b, n, d = x.shape
    h = CONV_HALO
    nch = d // tc
    hb = tm // h
    last = n // h - 1
    tok = lambda bi, i, j: (bi, i, 0)
    return pl.pallas_call(
        _conv_mix_kernel,
        grid=(b, n // tm, nch),
        in_specs=[pl.BlockSpec((1, tm, d), tok),
                  pl.BlockSpec((1, h, d), lambda bi, i, j: (bi, jnp.maximum(i * hb - 1, 0), 0)),
                  pl.BlockSpec((1, h, d), lambda bi, i, j: (bi, jnp.minimum((i + 1) * hb, last), 0)),
                  pl.BlockSpec((1, MOD_ROWS, d), lambda bi, i, j: (bi, 0, 0)),
                  pl.BlockSpec((MOD_ROWS, d), lambda bi, i, j: (0, 0)),
                  pl.BlockSpec((d, tc), lambda bi, i, j: (0, j)),
                  pl.BlockSpec((d, tc), lambda bi, i, j: (0, nch + j)),
                  pl.BlockSpec((d, tc), lambda bi, i, j: (0, 2 * nch + j)),
                  pl.BlockSpec((MOD_ROWS, tc), lambda bi, i, j: (0, j)),
                  pl.BlockSpec((tc, d), lambda bi, i, j: (j, 0))],
        out_specs=pl.BlockSpec((1, tm, d), tok),
        out_shape=jax.ShapeDtypeStruct((b, n, d), F32),
        scratch_shapes=[pltpu.VMEM((tm + 2 * h, d), BF16),
                        pltpu.VMEM((tm + 2 * h, tc), F32),
                        pltpu.VMEM((tm, d), F32)],
        compiler_params=_params(("arbitrary", "arbitrary", "arbitrary")),
        name="conv_mixer",
    )(x, x, x, mod, gains, w_in, w_in, w_in, conv_w, w_out)


def _mlp_kernel(x_ref, mod_ref, g_ref, w1_ref, w2_ref, o_ref, u_sc, acc_sc):
    f = pl.program_id(2)

    @pl.when(f == 0)
    def _():
        u_sc[...] = _modulated(x_ref[0], g_ref[2:3, :], mod_ref[0, 3:4, :], mod_ref[0, 4:5, :])

    hid = jnp.maximum(jnp.dot(u_sc[...], w1_ref[...], preferred_element_type=F32), 0.0)
    part = jnp.dot((hid * hid).astype(BF16), w2_ref[...], preferred_element_type=F32)

    @pl.when(f == 0)
    def _():
        acc_sc[...] = part

    @pl.when(f > 0)
    def _():
        acc_sc[...] += part

    @pl.when(f == pl.num_programs(2) - 1)
    def _():
        o_ref[0] = x_ref[0] + mod_ref[0, 5:6, :] * (_rms(acc_sc[...]) * g_ref[3:4, :])


def _mlp(x, mod, gains, w1, w2, tm, tf):
    b, n, d = x.shape
    dff = w1.shape[1]
    tok = lambda bi, i, f: (bi, i, 0)
    return pl.pallas_call(
        _mlp_kernel,
        grid=(b, n // tm, dff // tf),
        in_specs=[pl.BlockSpec((1, tm, d), tok),
                  pl.BlockSpec((1, MOD_ROWS, d), lambda bi, i, f: (bi, 0, 0)),
                  pl.BlockSpec((MOD_ROWS, d), lambda bi, i, f: (0, 0)),
                  pl.BlockSpec((d, tf), lambda bi, i, f: (0, f)),
                  pl.BlockSpec((tf, d), lambda bi, i, f: (f, 0))],
        out_specs=pl.BlockSpec((1, tm, d), tok),
        out_shape=jax.ShapeDtypeStruct((b, n, d), F32),
        scratch_shapes=[pltpu.VMEM((tm, d), BF16), pltpu.VMEM((tm, d), F32)],
        compiler_params=_params(("arbitrary", "arbitrary", "arbitrary")),
        name="sq_relu_mlp",
    )(x, mod, gains, w1, w2)


def _rope_tables(n):
    rows = n // GRID_W
    row = jnp.repeat(jnp.arange(rows, dtype=F32), GRID_W)
    col = jnp.tile(jnp.arange(GRID_W, dtype=F32), rows)
    inv_freq = ROPE_THETA ** (-jnp.arange(0, ROPE_AXIS_DIM, 2, dtype=F32) / ROPE_AXIS_DIM)
    ar, ac = row[:, None] * inv_freq, col[:, None] * inv_freq
    cos = jnp.concatenate([jnp.cos(ar), jnp.cos(ar), jnp.cos(ac), jnp.cos(ac)], axis=1)
    sin = jnp.concatenate([-jnp.sin(ar), jnp.sin(ar), -jnp.sin(ac), jnp.sin(ac)], axis=1)
    return cos, sin


def _pad_rows(t):
    return jnp.pad(t, ((0, MOD_ROWS - t.shape[0]), (0, 0)))


def _tile(n, want):
    return min(n, want)


def kernel(x, c, ctx, c_ctx, w_mod, b_mod, norm_gains, att_w_in, att_sink, att_w_out,
           conv_w_in, conv_w, conv_w_out, mlp_w1, mlp_w2):
    bsz, n_tok, d = x.shape
    n_ctx = ctx.shape[1]
    depth = w_mod.shape[0]

    cc = _pad_rows(jnp.concatenate([c, c_ctx[None, :]], axis=0))
    mod_all = _modulation(cc, w_mod, b_mod)
    cos_x, sin_x = _rope_tables(n_tok)
    cos_c, sin_c = jnp.ones((n_ctx, HEAD_DIM), F32), jnp.zeros((n_ctx, HEAD_DIM), F32)
    cs = _channel_dft_table()

    tm_x = _tile(n_tok, 512)
    h_ctx = ctx
    for l in range(depth):
        gains = _pad_rows(norm_gains[l])
        mod_x = jnp.stack([_pad_rows(mod_all[l, bi].reshape(N_MOD, d)) for bi in range(bsz)])
        mod_c1 = _pad_rows(mod_all[l, bsz].reshape(N_MOD, d))[None]
        mod_c = jnp.broadcast_to(mod_c1, (bsz, MOD_ROWS, d))
        ctx_update = any(j % 2 == 0 for j in range(l + 1, depth))
        if l % 2 == 0:
            a = l // 2
            w_in, w_out, sink = att_w_in[a].astype(BF16), att_w_out[a].astype(BF16), att_sink[a]
            u_c, q_c, k_c, v_c = _even_in(h_ctx, mod_c, gains, w_in, cs, cos_c, sin_c, n_ctx)
            u_x, q_x, k_x, v_x = _even_in(x, mod_x, gains, w_in, cs, cos_x, sin_x, tm_x)
            attn_x = _window_attention(q_x, k_x, v_x, k_c, v_c, sink, _tile(n_tok, 512))
            x = _even_out(_fourier_seq(u_x), attn_x, w_out, x, mod_x, gains, tm_x)
            if ctx_update:
                attn_c = _ctx_attention(q_c, k_c, v_c, sink)
                h_ctx = _even_out(_fourier_ctx(u_c), attn_c, w_out, h_ctx, mod_c, gains, n_ctx)
        else:
            m = l // 2
            w_in, w_out = conv_w_in[m].astype(BF16), conv_w_out[m].astype(BF16)
            cw = _pad_rows(conv_w[m])
            x = _conv_mix(x, mod_x, gains, w_in, cw, w_out, tm_x, 512)
            if ctx_update:
                h_ctx = _conv_mix(h_ctx, mod_c, gains, w_in, cw, w_out, n_ctx, 512)
        w1, w2 = mlp_w1[l].astype(BF16), mlp_w2[l].astype(BF16)
        x = _mlp(x, mod_x, gains, w1, w2, tm_x, 1024)
        if ctx_update:
            flat = _mlp(h_ctx.reshape(1, bsz * n_ctx, d), mod_c1, gains, w1, w2, bsz * n_ctx, 1024)
            h_ctx = flat.reshape(bsz, n_ctx, d)
    return x
```

```python
import functools

import numpy as np
import jax
import jax.numpy as jnp
from jax import lax
from jax.experimental import pallas as pl
from jax.experimental.pallas import tpu as pltpu

F32 = jnp.float32
BF16 = jnp.bfloat16

D_MODEL = 2048
HEAD_DIM = 128
N_HEADS = 8
N_KV_HEADS = 2
GQA_GROUP = 4
Q_WIDTH = 1024
KV_WIDTH = 256
FOURIER_WIDTH = 1024
N_FOURIER_GROUPS = 4
FOURIER_GROUP_DIM = 256
EVEN_IN_WIDTH = 2560
BLOCK = 128
GRID_W = 64
ROPE_THETA = 10000.0
ROPE_AXIS_DIM = 64
ATTN_SCALE = HEAD_DIM ** -0.5
D_FF = 8192
N_MOD = 6
NORM_EPS = 1e-6
NEG_INF = -1e30
DEPTH = 4

MOD_ROWS = 8
DFT_COLS = 128
CONV_HALO = 16
LANES = 128
ROW_CHUNK = 16
VMEM_LIMIT = 56 * 1024 * 1024


def _params(sem):
    return pltpu.CompilerParams(dimension_semantics=sem, vmem_limit_bytes=VMEM_LIMIT)


def _rms(t):
    return t * lax.rsqrt(jnp.mean(t * t, axis=-1, keepdims=True) + NORM_EPS)


def _modulated(x, gain, shift, scale):
    return (_rms(x) * (gain * (1.0 + scale)) + shift).astype(BF16)


def _modulate_rows(src_ref, dst_ref, dst_row0, gain, shift, scale):
    rows = src_ref.shape[0]
    eff = gain * (1.0 + scale)

    def scoped(rstd_ref):
        _row_rstd(src_ref, rstd_ref)

        def body(r, carry):
            sl = _chunk(r)
            dst_ref[pl.ds(dst_row0 + sl.start, ROW_CHUNK), :] = (
                src_ref[sl, :] * _lanes(rstd_ref[sl, :], src_ref.shape[1]) * eff + shift).astype(BF16)
            return carry

        lax.fori_loop(0, rows // ROW_CHUNK, body, 0, unroll=2)

    pl.run_scoped(scoped, pltpu.VMEM((rows, LANES), F32))


def _gated_residual_rows(y_ref, x_ref, o_ref, gate, gain):
    rows = y_ref.shape[0]
    eff = gate * gain

    def scoped(rstd_ref):
        _row_rstd(y_ref, rstd_ref)

        def body(r, carry):
            sl = _chunk(r)
            o_ref[sl, :] = x_ref[sl, :] + y_ref[sl, :] * _lanes(rstd_ref[sl, :], y_ref.shape[1]) * eff
            return carry

        lax.fori_loop(0, rows // ROW_CHUNK, body, 0, unroll=2)

    pl.run_scoped(scoped, pltpu.VMEM((rows, LANES), F32))


def _lanes(t, width):
    return jnp.concatenate([t] * (width // LANES), axis=1)


def _chunk(r):
    return pl.ds(pl.multiple_of(r * ROW_CHUNK, ROW_CHUNK), ROW_CHUNK)


def _row_rstd(src_ref, rstd_ref):
    def body(r, carry):
        sl = _chunk(r)
        t = src_ref[sl, :]
        rstd = lax.rsqrt(jnp.mean(t * t, axis=-1, keepdims=True) + NORM_EPS)
        rstd_ref[sl, :] = jnp.broadcast_to(rstd, (ROW_CHUNK, LANES))
        return carry

    lax.fori_loop(0, src_ref.shape[0] // ROW_CHUNK, body, 0, unroll=True)


def _mod_kernel(c_ref, w_ref, b_ref, o_ref):
    cf = c_ref[...]
    s = (cf * jax.nn.sigmoid(cf)).astype(BF16)
    o_ref[0] = jnp.dot(s, w_ref[0].astype(BF16), preferred_element_type=F32) + b_ref[0]


def _modulation(cc, w_mod, b_mod):
    depth, d, n = w_mod.shape
    tn = 1024
    return pl.pallas_call(
        _mod_kernel,
        grid=(depth, n // tn),
        in_specs=[pl.BlockSpec((MOD_ROWS, d), lambda l, j: (0, 0)),
                  pl.BlockSpec((1, d, tn), lambda l, j: (l, 0, j)),
                  pl.BlockSpec((1, 1, tn), lambda l, j: (l, 0, j))],
        out_specs=pl.BlockSpec((1, MOD_ROWS, tn), lambda l, j: (l, 0, j)),
        out_shape=jax.ShapeDtypeStruct((depth, MOD_ROWS, n), F32),
        compiler_params=_params(("arbitrary", "arbitrary")),
        name="modulation",
    )(cc, w_mod, b_mod.reshape(depth, 1, n))


def _even_in_kernel(x_ref, mod_ref, g_ref, w_ref, cs_ref, cos_ref, sin_ref,
                    u_ref, q_ref, k_ref, v_ref, un_sc):
    tm = x_ref.shape[1]
    _modulate_rows(x_ref.at[0], un_sc, 0, g_ref[0:1, :], mod_ref[0, 0:1, :], mod_ref[0, 1:2, :])
    p = jnp.dot(un_sc[...], w_ref[...], preferred_element_type=F32)
    cs = cs_ref[...]
    gd = FOURIER_GROUP_DIM
    for g in range(N_FOURIER_GROUPS):
        t = jnp.dot(p[:, g * gd:(g + 1) * gd].astype(BF16), cs, preferred_element_type=F32)
        u_ref[0, :, g * gd:(g + 1) * gd] = t[:, :gd].astype(BF16)
        u_ref[0, :, FOURIER_WIDTH + g * gd:FOURIER_WIDTH + (g + 1) * gd] = t[:, gd:].astype(BF16)
    cos = cos_ref[...]
    sin = sin_ref[...]
    lane = lax.broadcasted_iota(jnp.int32, (tm, HEAD_DIM), 1)
    first_half = (lane % ROPE_AXIS_DIM) < (ROPE_AXIS_DIM // 2)

    def rope(h):
        partner = jnp.where(first_half, pltpu.roll(h, HEAD_DIM - 32, 1), pltpu.roll(h, 32, 1))
        return h * cos + partner * sin

    q0 = FOURIER_WIDTH
    for h in range(N_HEADS):
        hq = p[:, q0 + h * HEAD_DIM:q0 + (h + 1) * HEAD_DIM]
        q_ref[0, :, h * HEAD_DIM:(h + 1) * HEAD_DIM] = (rope(hq) * ATTN_SCALE).astype(BF16)
    k0 = FOURIER_WIDTH + Q_WIDTH
    for h in range(N_KV_HEADS):
        hk = p[:, k0 + h * HEAD_DIM:k0 + (h + 1) * HEAD_DIM]
        k_ref[0, :, h * HEAD_DIM:(h + 1) * HEAD_DIM] = rope(hk).astype(BF16)
    v_ref[0] = p[:, k0 + KV_WIDTH:].astype(BF16)


def _even_in(x, mod, gains, w_in, cs, cos, sin, tm):
    b, n, d = x.shape
    tok = lambda bi, i: (bi, i, 0)
    const = lambda bi, i: (0, 0)
    return pl.pallas_call(
        _even_in_kernel,
        grid=(b, n // tm),
        in_specs=[pl.BlockSpec((1, tm, d), tok),
                  pl.BlockSpec((1, MOD_ROWS, d), lambda bi, i: (bi, 0, 0)),
                  pl.BlockSpec((MOD_ROWS, d), const),
                  pl.BlockSpec((d, EVEN_IN_WIDTH), const),
                  pl.BlockSpec((FOURIER_GROUP_DIM, 2 * FOURIER_GROUP_DIM), const),
                  pl.BlockSpec((tm, HEAD_DIM), lambda bi, i: (i, 0)),
                  pl.BlockSpec((tm, HEAD_DIM), lambda bi, i: (i, 0))],
        out_specs=[pl.BlockSpec((1, tm, 2 * FOURIER_WIDTH), tok),
                   pl.BlockSpec((1, tm, Q_WIDTH), tok),
                   pl.BlockSpec((1, tm, KV_WIDTH), tok),
                   pl.BlockSpec((1, tm, KV_WIDTH), tok)],
        out_shape=[jax.ShapeDtypeStruct((b, n, 2 * FOURIER_WIDTH), BF16),
                   jax.ShapeDtypeStruct((b, n, Q_WIDTH), BF16),
                   jax.ShapeDtypeStruct((b, n, KV_WIDTH), BF16),
                   jax.ShapeDtypeStruct((b, n, KV_WIDTH), BF16)],
        scratch_shapes=[pltpu.VMEM((tm, d), BF16)],
        compiler_params=_params(("arbitrary", "arbitrary")),
        name="even_in_proj",
    )(x, mod, gains, w_in, cs, cos, sin)


def _dft1_kernel(u_ref, t_ref, o_ref):
    for j in range(u_ref.shape[1]):
        xs = jnp.concatenate([u_ref[0, j, :, :FOURIER_WIDTH], u_ref[0, j, :, FOURIER_WIDTH:]], axis=0)
        o_ref[0, j] = jnp.dot(t_ref[j], xs, preferred_element_type=F32).astype(BF16)


def _dft1(ut, tab, tb):
    b, nb, na, w = ut.shape
    return pl.pallas_call(
        _dft1_kernel,
        grid=(nb // tb, b),
        in_specs=[pl.BlockSpec((1, tb, na, w), lambda j, bi: (bi, j, 0, 0)),
                  pl.BlockSpec((tb, 2 * na, 2 * na), lambda j, bi: (j, 0, 0))],
        out_specs=pl.BlockSpec((1, tb, 2 * na, FOURIER_WIDTH), lambda j, bi: (bi, j, 0, 0)),
        out_shape=jax.ShapeDtypeStruct((b, nb, 2 * na, FOURIER_WIDTH), BF16),
        compiler_params=_params(("arbitrary", "arbitrary")),
        name="seq_dft_stage1",
    )(ut, tab)


def _dft2_kernel(a_ref, m_ref, o_ref):
    for j in range(a_ref.shape[1]):
        o_ref[0, j] = jnp.dot(m_ref[...], a_ref[0, j], preferred_element_type=F32).astype(BF16)


def _dft2(at, m2, td):
    b, na, k2, w = at.shape
    nb = k2 // 2
    return pl.pallas_call(
        _dft2_kernel,
        grid=(b, na // td),
        in_specs=[pl.BlockSpec((1, td, k2, w), lambda bi, j: (bi, j, 0, 0)),
                  pl.BlockSpec((nb, k2), lambda bi, j: (0, 0))],
        out_specs=pl.BlockSpec((1, td, nb, w), lambda bi, j: (bi, j, 0, 0)),
        out_shape=jax.ShapeDtypeStruct((b, na, nb, w), BF16),
        compiler_params=_params(("arbitrary", "arbitrary")),
        name="seq_dft_stage2",
    )(at, m2)


def _seq_dft_tables(n):
    nb = DFT_COLS
    na = n // nb
    a = np.arange(na)
    dd = np.arange(na)
    bb = np.arange(nb)
    ph = (dd[None, :, None] * (nb * a[None, None, :] + bb[:, None, None])) % n
    ang = 2.0 * np.pi * ph / n
    c1, s1 = np.cos(ang), np.sin(ang)
    tab1 = np.concatenate([np.concatenate([c1, -s1], axis=2),
                           np.concatenate([s1, c1], axis=2)], axis=1)
    cidx = np.arange(nb)
    ang2 = 2.0 * np.pi * ((cidx[:, None] * bb[None, :]) % nb) / nb
    m2 = np.concatenate([np.cos(ang2), -np.sin(ang2)], axis=1) * n ** -0.5
    return jnp.asarray(tab1, BF16), jnp.asarray(m2, BF16)


def _channel_dft_table():
    i = np.arange(FOURIER_GROUP_DIM)
    ang = 2.0 * np.pi * ((i[:, None] * i[None, :]) % FOURIER_GROUP_DIM) / FOURIER_GROUP_DIM
    cs = np.concatenate([np.cos(ang), np.sin(ang)], axis=1) * FOURIER_GROUP_DIM ** -0.5
    return jnp.asarray(cs, BF16)


def _fourier_seq(u):
    b, n, w = u.shape
    nb = DFT_COLS
    na = n // nb
    tab1, m2 = _seq_dft_tables(n)
    ut = u.reshape(b, na, nb, w).transpose(0, 2, 1, 3)
    s1 = _dft1(ut, tab1, tb=8)
    at = s1.reshape(b, nb, 2, na, FOURIER_WIDTH).transpose(0, 3, 2, 1, 4)
    at = at.reshape(b, na, 2 * nb, FOURIER_WIDTH)
    yt = _dft2(at, m2, td=8)
    return yt.transpose(0, 2, 1, 3).reshape(b, n, FOURIER_WIDTH)


def _ctx_dft_kernel(u_ref, m_ref, o_ref):
    n = u_ref.shape[1]
    y = jnp.dot(m_ref[:, :n], u_ref[0, :, :FOURIER_WIDTH], preferred_element_type=F32)
    y = y + jnp.dot(m_ref[:, n:], u_ref[0, :, FOURIER_WIDTH:], preferred_element_type=F32)
    o_ref[0] = y.astype(BF16)


def _fourier_ctx(u):
    b, n, w = u.shape
    i = np.arange(n)
    ang = 2.0 * np.pi * ((i[:, None] * i[None, :]) % n) / n
    m = jnp.asarray(np.concatenate([np.cos(ang), -np.sin(ang)], axis=1) * n ** -0.5, BF16)
    return pl.pallas_call(
        _ctx_dft_kernel,
        grid=(b,),
        in_specs=[pl.BlockSpec((1, n, w), lambda bi: (bi, 0, 0)),
                  pl.BlockSpec((n, 2 * n), lambda bi: (0, 0))],
        out_specs=pl.BlockSpec((1, n, FOURIER_WIDTH), lambda bi: (bi, 0, 0)),
        out_shape=jax.ShapeDtypeStruct((b, n, FOURIER_WIDTH), BF16),
        compiler_params=_params(("arbitrary",)),
        name="ctx_dft",
    )(u, m)


def _sink_column(sink_ref, kvh, rows):
    parts = [jnp.full((rows, 1), sink_ref[kvh * GQA_GROUP + g], F32) for g in range(GQA_GROUP)]
    return jnp.concatenate(parts, axis=0)


def _softmax_pv(s, sink_col, v):
    m = jnp.maximum(sink_col, jnp.max(s, axis=-1, keepdims=True))
    e = jnp.exp(s - m)
    den = jnp.exp(sink_col - m) + jnp.sum(e, axis=-1, keepdims=True)
    return jnp.dot(e.astype(BF16), v, preferred_element_type=F32) / den


def _window_attn_kernel(sink_ref, q_ref, km_ref, kp_ref, kn_ref, vm_ref, vp_ref, vn_ref,
                        kc_ref, vc_ref, o_ref):
    i = pl.program_id(1)
    kvh = pl.program_id(2)
    n_tiles = pl.num_programs(1)
    blocks = q_ref.shape[1] // BLOCK
    n_ctx = kc_ref.shape[1]
    rows = GQA_GROUP * BLOCK
    sink_col = _sink_column(sink_ref, kvh, BLOCK)
    qi = lax.broadcasted_iota(jnp.int32, (rows, 3 * BLOCK + n_ctx), 0) % BLOCK
    kj = lax.broadcasted_iota(jnp.int32, (rows, 3 * BLOCK + n_ctx), 1)

    def band(has_prev, has_next):
        lo = qi + jnp.where(has_prev, 0, BLOCK)
        hi = qi + jnp.where(has_next, BLOCK, 0)
        return ((kj >= lo) & (kj < BLOCK)) | ((kj >= BLOCK) & (kj <= hi)) | (kj >= 2 * BLOCK)

    for jb in range(blocks):
        sl = slice(jb * BLOCK, (jb + 1) * BLOCK)
        q4 = jnp.concatenate([q_ref[0, sl, g * HEAD_DIM:(g + 1) * HEAD_DIM] for g in range(GQA_GROUP)],
                             axis=0)
        if jb == 0:
            k_prev, v_prev, has_prev = kp_ref[0], vp_ref[0], i > 0
        else:
            psl = slice((jb - 1) * BLOCK, jb * BLOCK)
            k_prev, v_prev, has_prev = km_ref[0, psl], vm_ref[0, psl], True
        if jb == blocks - 1:
            k_next, v_next, has_next = kn_ref[0], vn_ref[0], i < n_tiles - 1
        else:
            nsl = slice((jb + 1) * BLOCK, (jb + 2) * BLOCK)
            k_next, v_next, has_next = km_ref[0, nsl], vm_ref[0, nsl], True
        keys = jnp.concatenate([k_prev, k_next, km_ref[0, sl], kc_ref[0]], axis=0)
        vals = jnp.concatenate([v_prev, v_next, vm_ref[0, sl], vc_ref[0]], axis=0)
        s = lax.dot_general(q4, keys, (((1,), (1,)), ((), ())), preferred_element_type=F32)
        s = jnp.where(band(has_prev, has_next), s, NEG_INF)
        o = _softmax_pv(s, sink_col, vals)
        for g in range(GQA_GROUP):
            o_ref[0, sl, g * HEAD_DIM:(g + 1) * HEAD_DIM] = o[g * BLOCK:(g + 1) * BLOCK].astype(BF16)


def _window_attention(q, k, v, kc, vc, sink, tq):
    b, n, _ = q.shape
    n_ctx = kc.shape[1]
    bpt = tq // BLOCK
    nblk = n // BLOCK
    main = lambda bi, i, h: (bi, i, h)
    prev = lambda bi, i, h: (bi, jnp.maximum(i * bpt - 1, 0), h)
    nxt = lambda bi, i, h: (bi, jnp.minimum((i + 1) * bpt, nblk - 1), h)
    ctx = lambda bi, i, h: (bi, 0, h)
    kv_main = pl.BlockSpec((1, tq, HEAD_DIM), main)
    kv_prev = pl.BlockSpec((1, BLOCK, HEAD_DIM), prev)
    kv_next = pl.BlockSpec((1, BLOCK, HEAD_DIM), nxt)
    kv_ctx = pl.BlockSpec((1, n_ctx, HEAD_DIM), ctx)
    return pl.pallas_call(
        _window_attn_kernel,
        grid=(b, n // tq, N_KV_HEADS),
        in_specs=[pl.BlockSpec(memory_space=pltpu.SMEM),
                  pl.BlockSpec((1, tq, GQA_GROUP * HEAD_DIM), main),
                  kv_main, kv_prev, kv_next, kv_main, kv_prev, kv_next, kv_ctx, kv_ctx],
        out_specs=pl.BlockSpec((1, tq, GQA_GROUP * HEAD_DIM), main),
        out_shape=jax.ShapeDtypeStruct((b, n, Q_WIDTH), BF16),
        compiler_params=_params(("arbitrary", "arbitrary", "arbitrary")),
        name="window_attention",
    )(sink, q, k, k, k, v, v, v, kc, vc)


def _ctx_attn_kernel(sink_ref, q_ref, k_ref, v_ref, o_ref):
    kvh = pl.program_id(1)
    n = q_ref.shape[1]
    q4 = jnp.concatenate([q_ref[0, :, g * HEAD_DIM:(g + 1) * HEAD_DIM] for g in range(GQA_GROUP)], axis=0)
    s = lax.dot_general(q4, k_ref[0], (((1,), (1,)), ((), ())), preferred_element_type=F32)
    o = _softmax_pv(s, _sink_column(sink_ref, kvh, n), v_ref[0])
    for g in range(GQA_GROUP):
        o_ref[0, :, g * HEAD_DIM:(g + 1) * HEAD_DIM] = o[g * n:(g + 1) * n].astype(BF16)


def _ctx_attention(q, k, v, sink):
    b, n, _ = q.shape
    hq = pl.BlockSpec((1, n, GQA_GROUP * HEAD_DIM), lambda bi, h: (bi, 0, h))
    hkv = pl.BlockSpec((1, n, HEAD_DIM), lambda bi, h: (bi, 0, h))
    return pl.pallas_call(
        _ctx_attn_kernel,
        grid=(b, N_KV_HEADS),
        in_specs=[pl.BlockSpec(memory_space=pltpu.SMEM), hq, hkv, hkv],
        out_specs=hq,
        out_shape=jax.ShapeDtypeStruct((b, n, Q_WIDTH), BF16),
        compiler_params=_params(("arbitrary", "arbitrary")),
        name="ctx_attention",
    )(sink, q, k, v)


def _even_out_kernel(f_ref, a_ref, w_ref, x_ref, mod_ref, g_ref, o_ref, mix_sc):
    mix_sc[...] = (jnp.dot(f_ref[0], w_ref[:FOURIER_WIDTH, :], preferred_element_type=F32)
                   + jnp.dot(a_ref[0], w_ref[FOURIER_WIDTH:, :], preferred_element_type=F32))
    _gated_residual_rows(mix_sc, x_ref.at[0], o_ref.at[0], mod_ref[0, 2:3, :], g_ref[1:2, :])


def _even_out(four, attn, w_out, x, mod, gains, tm):
    b, n, d = x.shape
    tok = lambda bi, i: (bi, i, 0)
    const = lambda bi, i: (0, 0)
    return pl.pallas_call(
        _even_out_kernel,
        grid=(b, n // tm),
        in_specs=[pl.BlockSpec((1, tm, FOURIER_WIDTH), tok),
                  pl.BlockSpec((1, tm, Q_WIDTH), tok),
                  pl.BlockSpec((FOURIER_WIDTH + Q_WIDTH, d), const),
                  pl.BlockSpec((1, tm, d), tok),
                  pl.BlockSpec((1, MOD_ROWS, d), lambda bi, i: (bi, 0, 0)),
                  pl.BlockSpec((MOD_ROWS, d), const)],
        out_specs=pl.BlockSpec((1, tm, d), tok),
        out_shape=jax.ShapeDtypeStruct((b, n, d), F32),
        scratch_shapes=[pltpu.VMEM((tm, d), F32)],
        compiler_params=_params(("arbitrary", "arbitrary")),
        name="even_out_proj",
    )(four, attn, w_out, x, mod, gains)


def _conv_mix_kernel(x_ref, xp_ref, xn_ref, mod_ref, g_ref, wb_ref, wc_ref, wx_ref, cw_ref, wo_ref,
                     o_ref, u_sc, z_sc, acc_sc):
    i = pl.program_id(1)
    j = pl.program_id(2)
    tm = x_ref.shape[1]
    h = CONV_HALO

    @pl.when(j == 0)
    def _():
        gain, shift, scale = g_ref[0:1, :], mod_ref[0, 0:1, :], mod_ref[0, 1:2, :]
        keep_prev = jnp.where(i > 0, 1.0, 0.0).astype(BF16)
        keep_next = jnp.where(i < pl.num_programs(1) - 1, 1.0, 0.0).astype(BF16)
        u_sc[0:h, :] = _modulated(xp_ref[0], gain, shift, scale) * keep_prev
        _modulate_rows(x_ref.at[0], u_sc, h, gain, shift, scale)
        u_sc[h + tm:, :] = _modulated(xn_ref[0], gain, shift, scale) * keep_next
        acc_sc[...] = jnp.zeros_like(acc_sc)

    u_all = u_sc[...]
    gb = jnp.dot(u_sc[h:h + tm, :], wb_ref[...], preferred_element_type=F32)
    z_sc[...] = (jnp.dot(u_all, wc_ref[...], preferred_element_type=F32)
                 * jnp.dot(u_all, wx_ref[...], preferred_element_type=F32))
    y = (cw_ref[0:1, :] * z_sc[h - 1:h - 1 + tm, :]
         + cw_ref[1:2, :] * z_sc[h:h + tm, :]
         + cw_ref[2:3, :] * z_sc[h + 1:h + 1 + tm, :])
    acc_sc[...] += jnp.dot((gb * y).astype(BF16), wo_ref[...], preferred_element_type=F32)

    @pl.when(j == pl.num_programs(2) - 1)
    def _():
        _gated_residual_rows(acc_sc, x_ref.at[0], o_ref.at[0], mod_ref[0, 2:3, :], g_ref[1:2, :])


def _conv_mix(x, mod, gains, w_in, conv_w, w_out, tm, tc):
    b, n, d = x.shape
    h = CONV_HALO
    nch = d // tc
    hb = tm // h
    last = n // h - 1
    tok = lambda bi, i, j: (bi, i, 0)
    return pl.pallas_call(
        _conv_mix_kernel,
        grid=(b, n // tm, nch),
        in_specs=[pl.BlockSpec((1, tm, d), tok),
                  pl.BlockSpec((1, h, d), lambda bi, i, j: (bi, jnp.maximum(i * hb - 1, 0), 0)),
                  pl.BlockSpec((1, h, d), lambda bi, i, j: (bi, jnp.minimum((i + 1) * hb, last), 0)),
                  pl.BlockSpec((1, MOD_ROWS, d), lambda bi, i, j: (bi, 0, 0)),
                  pl.BlockSpec((MOD_ROWS, d), lambda bi, i, j: (0, 0)),
                  pl.BlockSpec((d, tc), lambda bi, i, j: (0, j)),
                  pl.BlockSpec((d, tc), lambda bi, i, j: (0, nch + j)),
                  pl.BlockSpec((d, tc), lambda bi, i, j: (0, 2 * nch + j)),
                  pl.BlockSpec((MOD_ROWS, tc), lambda bi, i, j: (0, j)),
                  pl.BlockSpec((tc, d), lambda bi, i, j: (j, 0))],
        out_specs=pl.BlockSpec((1, tm, d), tok),
        out_shape=jax.ShapeDtypeStruct((b, n, d), F32),
        scratch_shapes=[pltpu.VMEM((tm + 2 * h, d), BF16),
                        pltpu.VMEM((tm + 2 * h, tc), F32),
                        pltpu.VMEM((tm, d), F32)],
        compiler_params=_params(("arbitrary", "arbitrary", "arbitrary")),
        name="conv_mixer",
    )(x, x, x, mod, gains, w_in, w_in, w_in, conv_w, w_out)


def _mlp_kernel(x_ref, mod_ref, g_ref, w1_ref, w2_ref, o_ref, u_sc, acc_sc):
    f = pl.program_id(2)

    @pl.when(f == 0)
    def _():
        _modulate_rows(x_ref.at[0], u_sc, 0, g_ref[2:3, :], mod_ref[0, 3:4, :], mod_ref[0, 4:5, :])
        acc_sc[...] = jnp.zeros_like(acc_sc)

    hid = jnp.maximum(jnp.dot(u_sc[...], w1_ref[...], preferred_element_type=F32), 0.0)
    acc_sc[...] += jnp.dot((hid * hid).astype(BF16), w2_ref[...], preferred_element_type=F32)

    @pl.when(f == pl.num_programs(2) - 1)
    def _():
        _gated_residual_rows(acc_sc, x_ref.at[0], o_ref.at[0], mod_ref[0, 5:6, :], g_ref[3:4, :])


def _mlp(x, mod, gains, w1, w2, tm, tf):
    b, n, d = x.shape
    dff = w1.shape[1]
    tok = lambda bi, i, f: (bi, i, 0)
    return pl.pallas_call(
        _mlp_kernel,
        grid=(b, n // tm, dff // tf),
        in_specs=[pl.BlockSpec((1, tm, d), tok),
                  pl.BlockSpec((1, MOD_ROWS, d), lambda bi, i, f: (bi, 0, 0)),
                  pl.BlockSpec((MOD_ROWS, d), lambda bi, i, f: (0, 0)),
                  pl.BlockSpec((d, tf), lambda bi, i, f: (0, f)),
                  pl.BlockSpec((tf, d), lambda bi, i, f: (f, 0))],
        out_specs=pl.BlockSpec((1, tm, d), tok),
        out_shape=jax.ShapeDtypeStruct((b, n, d), F32),
        scratch_shapes=[pltpu.VMEM((tm, d), BF16), pltpu.VMEM((tm, d), F32)],
        compiler_params=_params(("arbitrary", "arbitrary", "arbitrary")),
        name="sq_relu_mlp",
    )(x, mod, gains, w1, w2)


def _rope_tables(n):
    rows = n // GRID_W
    row = jnp.repeat(jnp.arange(rows, dtype=F32), GRID_W)
    col = jnp.tile(jnp.arange(GRID_W, dtype=F32), rows)
    inv_freq = ROPE_THETA ** (-jnp.arange(0, ROPE_AXIS_DIM, 2, dtype=F32) / ROPE_AXIS_DIM)
    ar, ac = row[:, None] * inv_freq, col[:, None] * inv_freq
    cos = jnp.concatenate([jnp.cos(ar), jnp.cos(ar), jnp.cos(ac), jnp.cos(ac)], axis=1)
    sin = jnp.concatenate([-jnp.sin(ar), jnp.sin(ar), -jnp.sin(ac), jnp.sin(ac)], axis=1)
    return cos, sin


def _pad_rows(t):
    return jnp.pad(t, ((0, MOD_ROWS - t.shape[0]), (0, 0)))


def _tile(n, want):
    return min(n, want)


def kernel(x, c, ctx, c_ctx, w_mod, b_mod, norm_gains, att_w_in, att_sink, att_w_out,
           conv_w_in, conv_w, conv_w_out, mlp_w1, mlp_w2):
    bsz, n_tok, d = x.shape
    n_ctx = ctx.shape[1]
    depth = w_mod.shape[0]

    cc = _pad_rows(jnp.concatenate([c, c_ctx[None, :]], axis=0))
    mod_all = _modulation(cc, w_mod, b_mod)
    cos_x, sin_x = _rope_tables(n_tok)
    cos_c, sin_c = jnp.ones((n_ctx, HEAD_DIM), F32), jnp.zeros((n_ctx, HEAD_DIM), F32)
    cs = _channel_dft_table()

    tm_x = _tile(n_tok, 512)
    h_ctx = ctx
    for l in range(depth):
        gains = _pad_rows(norm_gains[l])
        mod_x = jnp.stack([_pad_rows(mod_all[l, bi].reshape(N_MOD, d)) for bi in range(bsz)])
        mod_c1 = _pad_rows(mod_all[l, bsz].reshape(N_MOD, d))[None]
        mod_c = jnp.broadcast_to(mod_c1, (bsz, MOD_ROWS, d))
        ctx_update = any(j % 2 == 0 for j in range(l + 1, depth))
        if l % 2 == 0:
            a = l // 2
            w_in, w_out, sink = att_w_in[a].astype(BF16), att_w_out[a].astype(BF16), att_sink[a]
            u_c, q_c, k_c, v_c = _even_in(h_ctx, mod_c, gains, w_in, cs, cos_c, sin_c, n_ctx)
            u_x, q_x, k_x, v_x = _even_in(x, mod_x, gains, w_in, cs, cos_x, sin_x, tm_x)
            attn_x = _window_attention(q_x, k_x, v_x, k_c, v_c, sink, _tile(n_tok, 512))
            x = _even_out(_fourier_seq(u_x), attn_x, w_out, x, mod_x, gains, tm_x)
            if ctx_update:
                attn_c = _ctx_attention(q_c, k_c, v_c, sink)
                h_ctx = _even_out(_fourier_ctx(u_c), attn_c, w_out, h_ctx, mod_c, gains, n_ctx)
        else:
            m = l // 2
            w_in, w_out = conv_w_in[m].astype(BF16), conv_w_out[m].astype(BF16)
            cw = _pad_rows(conv_w[m])
            x = _conv_mix(x, mod_x, gains, w_in, cw, w_out, tm_x, 512)
            if ctx_update:
                h_ctx = _conv_mix(h_ctx, mod_c, gains, w_in, cw, w_out, n_ctx, 512)
        w1, w2 = mlp_w1[l].astype(BF16), mlp_w2[l].astype(BF16)
        x = _mlp(x, mod_x, gains, w1, w2, tm_x, 1024)
        if ctx_update:
            flat = _mlp(h_ctx.reshape(1, bsz * n_ctx, d), mod_c1, gains, w1, w2, bsz * n_ctx, 1024)
            h_ctx = flat.reshape(bsz, n_ctx, d)
    return x
```

```python
import functools

import numpy as np
import jax
import jax.numpy as jnp
from jax import lax
from jax.experimental import pallas as pl
from jax.experimental.pallas import tpu as pltpu

F32 = jnp.float32
BF16 = jnp.bfloat16
U32 = jnp.uint32

D_MODEL = 2048
HEAD_DIM = 128
N_HEADS = 8
N_KV_HEADS = 2
GQA_GROUP = 4
Q_WIDTH = 1024
KV_WIDTH = 256
FOURIER_WIDTH = 1024
N_FOURIER_GROUPS = 4
FOURIER_GROUP_DIM = 256
EVEN_IN_WIDTH = 2560
BLOCK = 128
GRID_W = 64
ROPE_THETA = 10000.0
ROPE_AXIS_DIM = 64
ATTN_SCALE = HEAD_DIM ** -0.5
D_FF = 8192
N_MOD = 6
NORM_EPS = 1e-6
NEG_INF = -1e30
DEPTH = 4

MOD_ROWS = 8
DFT_COLS = 128
CONV_HALO = 16
LANES = 128
SUBLANES = 8
ROW_CHUNK = 16
VMEM_LIMIT = 56 * 1024 * 1024


def _params(sem):
    return pltpu.CompilerParams(dimension_semantics=sem, vmem_limit_bytes=VMEM_LIMIT)


def _rms(t):
    return t * lax.rsqrt(jnp.mean(t * t, axis=-1, keepdims=True) + NORM_EPS)


def _modulated(x, gain, shift, scale):
    return (_rms(x) * (gain * (1.0 + scale)) + shift).astype(BF16)


def _pack_pair(lo, hi):
    lo_bits = lax.bitcast_convert_type(lo.astype(BF16).astype(F32), U32)
    hi_bits = lax.bitcast_convert_type(hi.astype(BF16).astype(F32), U32)
    return (lo_bits >> 16) | hi_bits


def _unpack_pair(w):
    lo = lax.bitcast_convert_type(w << 16, F32).astype(BF16)
    hi = lax.bitcast_convert_type(w & jnp.uint32(0xFFFF0000), F32).astype(BF16)
    return lo, hi


def _modulate_rows(src_ref, dst_ref, dst_row0, gain, shift, scale):
    rows = src_ref.shape[0]
    eff = gain * (1.0 + scale)

    def scoped(rstd_ref):
        _row_rstd(src_ref, rstd_ref)

        def body(r, carry):
            sl = _chunk(r)
            dst_ref[pl.ds(dst_row0 + sl.start, ROW_CHUNK), :] = (
                src_ref[sl, :] * _lanes(rstd_ref[sl, :], src_ref.shape[1]) * eff + shift).astype(BF16)
            return carry

        lax.fori_loop(0, rows // ROW_CHUNK, body, 0, unroll=2)

    pl.run_scoped(scoped, pltpu.VMEM((rows, LANES), F32))


def _gated_residual_rows(y_ref, x_ref, o_ref, gate, gain):
    rows = y_ref.shape[0]
    eff = gate * gain

    def scoped(rstd_ref):
        _row_rstd(y_ref, rstd_ref)

        def body(r, carry):
            sl = _chunk(r)
            o_ref[sl, :] = x_ref[sl, :] + y_ref[sl, :] * _lanes(rstd_ref[sl, :], y_ref.shape[1]) * eff
            return carry

        lax.fori_loop(0, rows // ROW_CHUNK, body, 0, unroll=2)

    pl.run_scoped(scoped, pltpu.VMEM((rows, LANES), F32))


def _lanes(t, width):
    return jnp.concatenate([t] * (width // LANES), axis=1)


def _chunk(r):
    return pl.ds(pl.multiple_of(r * ROW_CHUNK, ROW_CHUNK), ROW_CHUNK)


def _row_rstd(src_ref, rstd_ref):
    def body(r, carry):
        sl = _chunk(r)
        t = src_ref[sl, :]
        rstd = lax.rsqrt(jnp.mean(t * t, axis=-1, keepdims=True) + NORM_EPS)
        rstd_ref[sl, :] = jnp.broadcast_to(rstd, (ROW_CHUNK, LANES))
        return carry

    lax.fori_loop(0, src_ref.shape[0] // ROW_CHUNK, body, 0, unroll=True)


def _mod_kernel(c_ref, w_ref, b_ref, o_ref):
    cf = c_ref[...]
    s = (cf * jax.nn.sigmoid(cf)).astype(BF16)
    o_ref[0] = jnp.dot(s, w_ref[0].astype(BF16), preferred_element_type=F32) + b_ref[0]


def _modulation(cc, w_mod, b_mod):
    depth, d, n = w_mod.shape
    tn = 1024
    return pl.pallas_call(
        _mod_kernel,
        grid=(depth, n // tn),
        in_specs=[pl.BlockSpec((MOD_ROWS, d), lambda l, j: (0, 0)),
                  pl.BlockSpec((1, d, tn), lambda l, j: (l, 0, j)),
                  pl.BlockSpec((1, 1, tn), lambda l, j: (l, 0, j))],
        out_specs=pl.BlockSpec((1, MOD_ROWS, tn), lambda l, j: (l, 0, j)),
        out_shape=jax.ShapeDtypeStruct((depth, MOD_ROWS, n), F32),
        compiler_params=_params(("arbitrary", "arbitrary")),
        name="modulation",
    )(cc, w_mod, b_mod.reshape(depth, 1, n))


def _even_in_kernel(x_ref, mod_ref, g_ref, w_ref, cs_ref, cos_ref, sin_ref,
                    u_ref, q_ref, k_ref, v_ref):
    tm = x_ref.shape[1]
    u = _modulated(x_ref[0], g_ref[0:1, :], mod_ref[0, 0:1, :], mod_ref[0, 1:2, :])
    p = jnp.dot(u, w_ref[...], preferred_element_type=F32)
    cs = cs_ref[...]
    gd = FOURIER_GROUP_DIM
    for g in range(N_FOURIER_GROUPS):
        t = jnp.dot(p[:, g * gd:(g + 1) * gd].astype(BF16), cs, preferred_element_type=F32)
        u_ref[0, :, g * gd:(g + 1) * gd] = _pack_pair(t[:, :gd], t[:, gd:])
    cos = cos_ref[...]
    sin = sin_ref[...]
    lane = lax.broadcasted_iota(jnp.int32, (tm, HEAD_DIM), 1)
    first_half = (lane % ROPE_AXIS_DIM) < (ROPE_AXIS_DIM // 2)

    def rope(h):
        partner = jnp.where(first_half, pltpu.roll(h, HEAD_DIM - 32, 1), pltpu.roll(h, 32, 1))
        return h * cos + partner * sin

    q0 = FOURIER_WIDTH
    for h in range(N_HEADS):
        hq = p[:, q0 + h * HEAD_DIM:q0 + (h + 1) * HEAD_DIM]
        q_ref[0, :, h * HEAD_DIM:(h + 1) * HEAD_DIM] = (rope(hq) * ATTN_SCALE).astype(BF16)
    k0 = FOURIER_WIDTH + Q_WIDTH
    for h in range(N_KV_HEADS):
        hk = p[:, k0 + h * HEAD_DIM:k0 + (h + 1) * HEAD_DIM]
        k_ref[0, :, h * HEAD_DIM:(h + 1) * HEAD_DIM] = rope(hk).astype(BF16)
    v_ref[0] = p[:, k0 + KV_WIDTH:].astype(BF16)


def _even_in(x, mod, gains, w_in, cs, cos, sin, tm):
    b, n, d = x.shape
    tok = lambda bi, i: (bi, i, 0)
    const = lambda bi, i: (0, 0)
    return pl.pallas_call(
        _even_in_kernel,
        grid=(b, n // tm),
        in_specs=[pl.BlockSpec((1, tm, d), tok),
                  pl.BlockSpec((1, MOD_ROWS, d), lambda bi, i: (bi, 0, 0)),
                  pl.BlockSpec((MOD_ROWS, d), const),
                  pl.BlockSpec((d, EVEN_IN_WIDTH), const),
                  pl.BlockSpec((FOURIER_GROUP_DIM, 2 * FOURIER_GROUP_DIM), const),
                  pl.BlockSpec((tm, HEAD_DIM), lambda bi, i: (i, 0)),
                  pl.BlockSpec((tm, HEAD_DIM), lambda bi, i: (i, 0))],
        out_specs=[pl.BlockSpec((1, tm, FOURIER_WIDTH), tok),
                   pl.BlockSpec((1, tm, Q_WIDTH), tok),
                   pl.BlockSpec((1, tm, KV_WIDTH), tok),
                   pl.BlockSpec((1, tm, KV_WIDTH), tok)],
        out_shape=[jax.ShapeDtypeStruct((b, n, FOURIER_WIDTH), U32),
                   jax.ShapeDtypeStruct((b, n, Q_WIDTH), BF16),
                   jax.ShapeDtypeStruct((b, n, KV_WIDTH), BF16),
                   jax.ShapeDtypeStruct((b, n, KV_WIDTH), BF16)],
        compiler_params=_params(("arbitrary", "arbitrary")),
        name="even_in_proj",
    )(x, mod, gains, w_in, cs, cos, sin)


def _dft1_kernel(u_ref, t_ref, o_ref, w_sc):
    na = u_ref.shape[1]
    for j in range(SUBLANES):
        w_sc[...] = u_ref[0, :, j, :]
        uc, us = _unpack_pair(w_sc[...])
        res = jnp.dot(t_ref[j], jnp.concatenate([uc, us], axis=0), preferred_element_type=F32)
        o_ref[0, j] = _pack_pair(res[:na], res[na:])


def _dft1(u4, tab):
    b, na, nb, w = u4.shape
    return pl.pallas_call(
        _dft1_kernel,
        grid=(nb // SUBLANES, b),
        in_specs=[pl.BlockSpec((1, na, SUBLANES, w), lambda j, bi: (bi, 0, j, 0)),
                  pl.BlockSpec((SUBLANES, 2 * na, 2 * na), lambda j, bi: (j, 0, 0))],
        out_specs=pl.BlockSpec((1, SUBLANES, na, w), lambda j, bi: (bi, j, 0, 0)),
        out_shape=jax.ShapeDtypeStruct((b, nb, na, w), U32),
        scratch_shapes=[pltpu.VMEM((na, w), U32)],
        compiler_params=_params(("arbitrary", "arbitrary")),
        name="seq_dft_stage1",
    )(u4, tab)


def _dft2_kernel(a_ref, m_ref, o_ref, w_sc):
    half = a_ref.shape[3] // 2
    for j in range(SUBLANES):
        w_sc[...] = a_ref[0, :, j, :]
        ar, ai = _unpack_pair(w_sc[...])
        y = jnp.dot(m_ref[...], jnp.concatenate([ar, ai], axis=0), preferred_element_type=F32)
        o_ref[0, j] = _pack_pair(y[:, :half], y[:, half:])


def _dft2(a4, m2):
    b, nb, na, w = a4.shape
    return pl.pallas_call(
        _dft2_kernel,
        grid=(b, na // SUBLANES),
        in_specs=[pl.BlockSpec((1, nb, SUBLANES, w), lambda bi, j: (bi, 0, j, 0)),
                  pl.BlockSpec((nb, 2 * nb), lambda bi, j: (0, 0))],
        out_specs=pl.BlockSpec((1, SUBLANES, nb, w // 2), lambda bi, j: (bi, j, 0, 0)),
        out_shape=jax.ShapeDtypeStruct((b, na, nb, w // 2), U32),
        scratch_shapes=[pltpu.VMEM((nb, w), U32)],
        compiler_params=_params(("arbitrary", "arbitrary")),
        name="seq_dft_stage2",
    )(a4, m2)


def _seq_dft_tables(n):
    nb = DFT_COLS
    na = n // nb
    a = np.arange(na)
    dd = np.arange(na)
    bb = np.arange(nb)
    ph = (dd[None, :, None] * (nb * a[None, None, :] + bb[:, None, None])) % n
    ang = 2.0 * np.pi * ph / n
    c1, s1 = np.cos(ang), np.sin(ang)
    tab1 = np.concatenate([np.concatenate([c1, -s1], axis=2),
                           np.concatenate([s1, c1], axis=2)], axis=1)
    cidx = np.arange(nb)
    ang2 = 2.0 * np.pi * ((cidx[:, None] * bb[None, :]) % nb) / nb
    m2 = np.concatenate([np.cos(ang2), -np.sin(ang2)], axis=1) * n ** -0.5
    return jnp.asarray(tab1, BF16), jnp.asarray(m2, BF16)


def _channel_dft_table():
    i = np.arange(FOURIER_GROUP_DIM)
    ang = 2.0 * np.pi * ((i[:, None] * i[None, :]) % FOURIER_GROUP_DIM) / FOURIER_GROUP_DIM
    cs = np.concatenate([np.cos(ang), np.sin(ang)], axis=1) * FOURIER_GROUP_DIM ** -0.5
    return jnp.asarray(cs, BF16)


def _fourier_seq(u):
    b, n, w = u.shape
    nb = DFT_COLS
    na = n // nb
    tab1, m2 = _seq_dft_tables(n)
    return _dft2(_dft1(u.reshape(b, na, nb, w), tab1), m2)


def _ctx_dft_kernel(u_ref, m_ref, o_ref):
    n = u_ref.shape[1]
    half = u_ref.shape[2] // 2
    uc, us = _unpack_pair(u_ref[0])
    y = jnp.dot(m_ref[...], jnp.concatenate([uc, us], axis=0), preferred_element_type=F32)
    o_ref[0] = _pack_pair(y[:, :half], y[:, half:])


def _fourier_ctx(u):
    b, n, w = u.shape
    i = np.arange(n)
    ang = 2.0 * np.pi * ((i[:, None] * i[None, :]) % n) / n
    m = jnp.asarray(np.concatenate([np.cos(ang), -np.sin(ang)], axis=1) * n ** -0.5, BF16)
    return pl.pallas_call(
        _ctx_dft_kernel,
        grid=(b,),
        in_specs=[pl.BlockSpec((1, n, w), lambda bi: (bi, 0, 0)),
                  pl.BlockSpec((n, 2 * n), lambda bi: (0, 0))],
        out_specs=pl.BlockSpec((1, n, w // 2), lambda bi: (bi, 0, 0)),
        out_shape=jax.ShapeDtypeStruct((b, n, w // 2), U32),
        compiler_params=_params(("arbitrary",)),
        name="ctx_dft",
    )(u, m)


def _sink_column(sink_ref, kvh, rows):
    parts = [jnp.full((rows, 1), sink_ref[kvh * GQA_GROUP + g], F32) for g in range(GQA_GROUP)]
    return jnp.concatenate(parts, axis=0)


def _softmax_pv(s, sink_col, v):
    m = jnp.maximum(sink_col, jnp.max(s, axis=-1, keepdims=True))
    e = jnp.exp(s - m)
    den = jnp.exp(sink_col - m) + jnp.sum(e, axis=-1, keepdims=True)
    return jnp.dot(e.astype(BF16), v, preferred_element_type=F32) / den


def _window_attn_kernel(sink_ref, q_ref, km_ref, kp_ref, kn_ref, vm_ref, vp_ref, vn_ref,
                        kc_ref, vc_ref, o_ref):
    i = pl.program_id(1)
    kvh = pl.program_id(2)
    n_tiles = pl.num_programs(1)
    blocks = q_ref.shape[1] // BLOCK
    n_ctx = kc_ref.shape[1]
    rows = GQA_GROUP * BLOCK
    sink_col = _sink_column(sink_ref, kvh, BLOCK)
    qi = lax.broadcasted_iota(jnp.int32, (rows, 3 * BLOCK + n_ctx), 0) % BLOCK
    kj = lax.broadcasted_iota(jnp.int32, (rows, 3 * BLOCK + n_ctx), 1)

    def band(has_prev, has_next):
        lo = qi + jnp.where(has_prev, 0, BLOCK)
        hi = qi + jnp.where(has_next, BLOCK, 0)
        return ((kj >= lo) & (kj < BLOCK)) | ((kj >= BLOCK) & (kj <= hi)) | (kj >= 2 * BLOCK)

    for jb in range(blocks):
        sl = slice(jb * BLOCK, (jb + 1) * BLOCK)
        q4 = jnp.concatenate([q_ref[0, sl, g * HEAD_DIM:(g + 1) * HEAD_DIM] for g in range(GQA_GROUP)],
                             axis=0)
        if jb == 0:
            k_prev, v_prev, has_prev = kp_ref[0], vp_ref[0], i > 0
        else:
            psl = slice((jb - 1) * BLOCK, jb * BLOCK)
            k_prev, v_prev, has_prev = km_ref[0, psl], vm_ref[0, psl], True
        if jb == blocks - 1:
            k_next, v_next, has_next = kn_ref[0], vn_ref[0], i < n_tiles - 1
        else:
            nsl = slice((jb + 1) * BLOCK, (jb + 2) * BLOCK)
            k_next, v_next, has_next = km_ref[0, nsl], vm_ref[0, nsl], True
        keys = jnp.concatenate([k_prev, k_next, km_ref[0, sl], kc_ref[0]], axis=0)
        vals = jnp.concatenate([v_prev, v_next, vm_ref[0, sl], vc_ref[0]], axis=0)
        s = lax.dot_general(q4, keys, (((1,), (1,)), ((), ())), preferred_element_type=F32)
        s = jnp.where(band(has_prev, has_next), s, NEG_INF)
        o = _softmax_pv(s, sink_col, vals)
        for g in range(GQA_GROUP):
            o_ref[0, sl, g * HEAD_DIM:(g + 1) * HEAD_DIM] = o[g * BLOCK:(g + 1) * BLOCK].astype(BF16)


def _window_attention(q, k, v, kc, vc, sink, tq):
    b, n, _ = q.shape
    n_ctx = kc.shape[1]
    bpt = tq // BLOCK
    nblk = n // BLOCK
    main = lambda bi, i, h: (bi, i, h)
    prev = lambda bi, i, h: (bi, jnp.maximum(i * bpt - 1, 0), h)
    nxt = lambda bi, i, h: (bi, jnp.minimum((i + 1) * bpt, nblk - 1), h)
    ctx = lambda bi, i, h: (bi, 0, h)
    kv_main = pl.BlockSpec((1, tq, HEAD_DIM), main)
    kv_prev = pl.BlockSpec((1, BLOCK, HEAD_DIM), prev)
    kv_next = pl.BlockSpec((1, BLOCK, HEAD_DIM), nxt)
    kv_ctx = pl.BlockSpec((1, n_ctx, HEAD_DIM), ctx)
    return pl.pallas_call(
        _window_attn_kernel,
        grid=(b, n // tq, N_KV_HEADS),
        in_specs=[pl.BlockSpec(memory_space=pltpu.SMEM),
                  pl.BlockSpec((1, tq, GQA_GROUP * HEAD_DIM), main),
                  kv_main, kv_prev, kv_next, kv_main, kv_prev, kv_next, kv_ctx, kv_ctx],
        out_specs=pl.BlockSpec((1, tq, GQA_GROUP * HEAD_DIM), main),
        out_shape=jax.ShapeDtypeStruct((b, n, Q_WIDTH), BF16),
        compiler_params=_params(("arbitrary", "arbitrary", "arbitrary")),
        name="window_attention",
    )(sink, q, k, k, k, v, v, v, kc, vc)


def _ctx_attn_kernel(sink_ref, q_ref, k_ref, v_ref, o_ref):
    kvh = pl.program_id(1)
    n = q_ref.shape[1]
    q4 = jnp.concatenate([q_ref[0, :, g * HEAD_DIM:(g + 1) * HEAD_DIM] for g in range(GQA_GROUP)], axis=0)
    s = lax.dot_general(q4, k_ref[0], (((1,), (1,)), ((), ())), preferred_element_type=F32)
    o = _softmax_pv(s, _sink_column(sink_ref, kvh, n), v_ref[0])
    for g in range(GQA_GROUP):
        o_ref[0, :, g * HEAD_DIM:(g + 1) * HEAD_DIM] = o[g * n:(g + 1) * n].astype(BF16)


def _ctx_attention(q, k, v, sink):
    b, n, _ = q.shape
    hq = pl.BlockSpec((1, n, GQA_GROUP * HEAD_DIM), lambda bi, h: (bi, 0, h))
    hkv = pl.BlockSpec((1, n, HEAD_DIM), lambda bi, h: (bi, 0, h))
    return pl.pallas_call(
        _ctx_attn_kernel,
        grid=(b, N_KV_HEADS),
        in_specs=[pl.BlockSpec(memory_space=pltpu.SMEM), hq, hkv, hkv],
        out_specs=hq,
        out_shape=jax.ShapeDtypeStruct((b, n, Q_WIDTH), BF16),
        compiler_params=_params(("arbitrary", "arbitrary")),
        name="ctx_attention",
    )(sink, q, k, v)


def _even_out_kernel(f_ref, a_ref, w_ref, x_ref, mod_ref, g_ref, o_ref, w_sc, *, cols_per_tile):
    half = FOURIER_WIDTH // 2
    if cols_per_tile:
        na = f_ref.shape[1]
        c0 = (pl.program_id(1) * cols_per_tile) % SUBLANES
        for jc in range(cols_per_tile):
            w_sc[jc * na:(jc + 1) * na, :] = f_ref[0, :, c0 + jc, :]
    else:
        w_sc[...] = f_ref[0]
    f_lo, f_hi = _unpack_pair(w_sc[...])
    mix = jnp.dot(f_lo, w_ref[:half, :], preferred_element_type=F32)
    mix = mix + jnp.dot(f_hi, w_ref[half:FOURIER_WIDTH, :], preferred_element_type=F32)
    mix = mix + jnp.dot(a_ref[0], w_ref[FOURIER_WIDTH:, :], preferred_element_type=F32)
    o_ref[0] = x_ref[0] + mod_ref[0, 2:3, :] * (_rms(mix) * g_ref[1:2, :])


def _even_out(four, attn, w_out, x, mod, gains, tm):
    b, n, d = x.shape
    tok = lambda bi, i: (bi, i, 0)
    const = lambda bi, i: (0, 0)
    half = FOURIER_WIDTH // 2
    if four.ndim == 4:
        na = four.shape[1]
        cols_per_tile = tm // na
        assert tm == cols_per_tile * na and SUBLANES % cols_per_tile == 0
        four_spec = pl.BlockSpec((1, na, SUBLANES, half),
                                 lambda bi, i: (bi, 0, (i * cols_per_tile) // SUBLANES, 0))
    else:
        cols_per_tile = 0
        four_spec = pl.BlockSpec((1, tm, half), tok)
    return pl.pallas_call(
        functools.partial(_even_out_kernel, cols_per_tile=cols_per_tile),
        grid=(b, n // tm),
        in_specs=[four_spec,
                  pl.BlockSpec((1, tm, Q_WIDTH), tok),
                  pl.BlockSpec((FOURIER_WIDTH + Q_WIDTH, d), const),
                  pl.BlockSpec((1, tm, d), tok),
                  pl.BlockSpec((1, MOD_ROWS, d), lambda bi, i: (bi, 0, 0)),
                  pl.BlockSpec((MOD_ROWS, d), const)],
        out_specs=pl.BlockSpec((1, tm, d), tok),
        out_shape=jax.ShapeDtypeStruct((b, n, d), F32),
        scratch_shapes=[pltpu.VMEM((tm, half), U32)],
        compiler_params=_params(("arbitrary", "arbitrary")),
        name="even_out_proj",
    )(four, attn, w_out, x, mod, gains)


def _conv_mix_kernel(x_ref, xp_ref, xn_ref, mod_ref, g_ref, wb_ref, wc_ref, wx_ref, cw_ref, wo_ref,
                     o_ref, u_sc, z_sc, acc_sc):
    i = pl.program_id(1)
    j = pl.program_id(2)
    tm = x_ref.shape[1]
    h = CONV_HALO

    @pl.when(j == 0)
    def _():
        gain, shift, scale = g_ref[0:1, :], mod_ref[0, 0:1, :], mod_ref[0, 1:2, :]
        keep_prev = jnp.where(i > 0, 1.0, 0.0).astype(BF16)
        keep_next = jnp.where(i < pl.num_programs(1) - 1, 1.0, 0.0).astype(BF16)
        u_sc[0:h, :] = _modulated(xp_ref[0], gain, shift, scale) * keep_prev
        _modulate_rows(x_ref.at[0], u_sc, h, gain, shift, scale)
        u_sc[h + tm:, :] = _modulated(xn_ref[0], gain, shift, scale) * keep_next
        acc_sc[...] = jnp.zeros_like(acc_sc)

    u_all = u_sc[...]
    gb = jnp.dot(u_sc[h:h + tm, :], wb_ref[...], preferred_element_type=F32)
    z_sc[...] = (jnp.dot(u_all, wc_ref[...], preferred_element_type=F32)
                 * jnp.dot(u_all, wx_ref[...], preferred_element_type=F32))
    y = (cw_ref[0:1, :] * z_sc[h - 1:h - 1 + tm, :]
         + cw_ref[1:2, :] * z_sc[h:h + tm, :]
         + cw_ref[2:3, :] * z_sc[h + 1:h + 1 + tm, :])
    acc_sc[...] += jnp.dot((gb * y).astype(BF16), wo_ref[...], preferred_element_type=F32)

    @pl.when(j == pl.num_programs(2) - 1)
    def _():
        _gated_residual_rows(acc_sc, x_ref.at[0], o_ref.at[0], mod_ref[0, 2:3, :], g_ref[1:2, :])


def _conv_mix(x, mod, gains, w_in, conv_w, w_out, tm, tc):
    b, n, d = x.shape
    h = CONV_HALO
    nch = d // tc
    hb = tm // h
    last = n // h - 1
    tok = lambda bi, i, j: (bi, i, 0)
    return pl.pallas_call(
        _conv_mix_kernel,
        grid=(b, n // tm, nch),
        in_specs=[pl.BlockSpec((1, tm, d), tok),
                  pl.BlockSpec((1, h, d), lambda bi, i, j: (bi, jnp.maximum(i * hb - 1, 0), 0)),
                  pl.BlockSpec((1, h, d), lambda bi, i, j: (bi, jnp.minimum((i + 1) * hb, last), 0)),
                  pl.BlockSpec((1, MOD_ROWS, d), lambda bi, i, j: (bi, 0, 0)),
                  pl.BlockSpec((MOD_ROWS, d), lambda bi, i, j: (0, 0)),
                  pl.BlockSpec((d, tc), lambda bi, i, j: (0, j)),
                  pl.BlockSpec((d, tc), lambda bi, i, j: (0, nch + j)),
                  pl.BlockSpec((d, tc), lambda bi, i, j: (0, 2 * nch + j)),
                  pl.BlockSpec((MOD_ROWS, tc), lambda bi, i, j: (0, j)),
                  pl.BlockSpec((tc, d), lambda bi, i, j: (j, 0))],
        out_specs=pl.BlockSpec((1, tm, d), tok),
        out_shape=jax.ShapeDtypeStruct((b, n, d), F32),
        scratch_shapes=[pltpu.VMEM((tm + 2 * h, d), BF16),
                        pltpu.VMEM((tm + 2 * h, tc), F32),
                        pltpu.VMEM((tm, d), F32)],
        compiler_params=_params(("arbitrary", "arbitrary", "arbitrary")),
        name="conv_mixer",
    )(x, x, x, mod, gains, w_in, w_in, w_in, conv_w, w_out)


def _mlp_kernel(x_ref, mod_ref, g_ref, w1_ref, w2_ref, o_ref, u_sc, acc_sc):
    f = pl.program_id(2)

    @pl.when(f == 0)
    def _():
        _modulate_rows(x_ref.at[0], u_sc, 0, g_ref[2:3, :], mod_ref[0, 3:4, :], mod_ref[0, 4:5, :])
        acc_sc[...] = jnp.zeros_like(acc_sc)

    hid = jnp.maximum(jnp.dot(u_sc[...], w1_ref[...], preferred_element_type=F32), 0.0)
    acc_sc[...] += jnp.dot((hid * hid).astype(BF16), w2_ref[...], preferred_element_type=F32)

    @pl.when(f == pl.num_programs(2) - 1)
    def _():
        _gated_residual_rows(acc_sc, x_ref.at[0], o_ref.at[0], mod_ref[0, 5:6, :], g_ref[3:4, :])


def _mlp(x, mod, gains, w1, w2, tm, tf):
    b, n, d = x.shape
    dff = w1.shape[1]
    tok = lambda bi, i, f: (bi, i, 0)
    return pl.pallas_call(
        _mlp_kernel,
        grid=(b, n // tm, dff // tf),
        in_specs=[pl.BlockSpec((1, tm, d), tok),
                  pl.BlockSpec((1, MOD_ROWS, d), lambda bi, i, f: (bi, 0, 0)),
                  pl.BlockSpec((MOD_ROWS, d), lambda bi, i, f: (0, 0)),
                  pl.BlockSpec((d, tf), lambda bi, i, f: (0, f)),
                  pl.BlockSpec((tf, d), lambda bi, i, f: (f, 0))],
        out_specs=pl.BlockSpec((1, tm, d), tok),
        out_shape=jax.ShapeDtypeStruct((b, n, d), F32),
        scratch_shapes=[pltpu.VMEM((tm, d), BF16), pltpu.VMEM((tm, d), F32)],
        compiler_params=_params(("arbitrary", "arbitrary", "arbitrary")),
        name="sq_relu_mlp",
    )(x, mod, gains, w1, w2)


def _rope_tables(n):
    rows = n // GRID_W
    row = jnp.repeat(jnp.arange(rows, dtype=F32), GRID_W)
    col = jnp.tile(jnp.arange(GRID_W, dtype=F32), rows)
    inv_freq = ROPE_THETA ** (-jnp.arange(0, ROPE_AXIS_DIM, 2, dtype=F32) / ROPE_AXIS_DIM)
    ar, ac = row[:, None] * inv_freq, col[:, None] * inv_freq
    cos = jnp.concatenate([jnp.cos(ar), jnp.cos(ar), jnp.cos(ac), jnp.cos(ac)], axis=1)
    sin = jnp.concatenate([-jnp.sin(ar), jnp.sin(ar), -jnp.sin(ac), jnp.sin(ac)], axis=1)
    return cos, sin


def _pad_rows(t):
    return jnp.pad(t, ((0, MOD_ROWS - t.shape[0]), (0, 0)))


def _tile(n, want):
    return min(n, want)


def kernel(x, c, ctx, c_ctx, w_mod, b_mod, norm_gains, att_w_in, att_sink, att_w_out,
           conv_w_in, conv_w, conv_w_out, mlp_w1, mlp_w2):
    bsz, n_tok, d = x.shape
    n_ctx = ctx.shape[1]
    depth = w_mod.shape[0]

    cc = _pad_rows(jnp.concatenate([c, c_ctx[None, :]], axis=0))
    mod_all = _modulation(cc, w_mod, b_mod)
    cos_x, sin_x = _rope_tables(n_tok)
    cos_c, sin_c = jnp.ones((n_ctx, HEAD_DIM), F32), jnp.zeros((n_ctx, HEAD_DIM), F32)
    cs = _channel_dft_table()

    tm_x = _tile(n_tok, 512)
    h_ctx = ctx
    for l in range(depth):
        gains = _pad_rows(norm_gains[l])
        mod_x = jnp.stack([_pad_rows(mod_all[l, bi].reshape(N_MOD, d)) for bi in range(bsz)])
        mod_c1 = _pad_rows(mod_all[l, bsz].reshape(N_MOD, d))[None]
        mod_c = jnp.broadcast_to(mod_c1, (bsz, MOD_ROWS, d))
        ctx_update = any(j % 2 == 0 for j in range(l + 1, depth))
        if l % 2 == 0:
            a = l // 2
            w_in, w_out, sink = att_w_in[a].astype(BF16), att_w_out[a].astype(BF16), att_sink[a]
            u_c, q_c, k_c, v_c = _even_in(h_ctx, mod_c, gains, w_in, cs, cos_c, sin_c, n_ctx)
            u_x, q_x, k_x, v_x = _even_in(x, mod_x, gains, w_in, cs, cos_x, sin_x, tm_x)
            attn_x = _window_attention(q_x, k_x, v_x, k_c, v_c, sink, _tile(n_tok, 512))
            x = _even_out(_fourier_seq(u_x), attn_x, w_out, x, mod_x, gains, 4 * (n_tok // DFT_COLS))
            if ctx_update:
                attn_c = _ctx_attention(q_c, k_c, v_c, sink)
                h_ctx = _even_out(_fourier_ctx(u_c), attn_c, w_out, h_ctx, mod_c, gains, n_ctx)
        else:
            m = l // 2
            w_in, w_out = conv_w_in[m].astype(BF16), conv_w_out[m].astype(BF16)
            cw = _pad_rows(conv_w[m])
            x = _conv_mix(x, mod_x, gains, w_in, cw, w_out, tm_x, 512)
            if ctx_update:
                h_ctx = _conv_mix(h_ctx, mod_c, gains, w_in, cw, w_out, n_ctx, 512)
        w1, w2 = mlp_w1[l].astype(BF16), mlp_w2[l].astype(BF16)
        x = _mlp(x, mod_x, gains, w1, w2, tm_x, 1024)
        if ctx_update:
            flat = _mlp(h_ctx.reshape(1, bsz * n_ctx, d), mod_c1, gains, w1, w2, bsz * n_ctx, 1024)
            h_ctx = flat.reshape(bsz, n_ctx, d)
    return x
```

```python
import functools

import numpy as np
import jax
import jax.numpy as jnp
from jax import lax
from jax.experimental import pallas as pl
from jax.experimental.pallas import tpu as pltpu

F32 = jnp.float32
BF16 = jnp.bfloat16
U32 = jnp.uint32

D_MODEL = 2048
HEAD_DIM = 128
N_HEADS = 8
N_KV_HEADS = 2
GQA_GROUP = 4
Q_WIDTH = 1024
KV_WIDTH = 256
FOURIER_WIDTH = 1024
N_FOURIER_GROUPS = 4
FOURIER_GROUP_DIM = 256
EVEN_IN_WIDTH = 2560
BLOCK = 128
GRID_W = 64
ROPE_THETA = 10000.0
ROPE_AXIS_DIM = 64
ATTN_SCALE = HEAD_DIM ** -0.5
LOG2_E = 1.4426950408889634
D_FF = 8192
N_MOD = 6
NORM_EPS = 1e-6
NEG_INF = -1e30
DEPTH = 4

MOD_ROWS = 8
DFT_COLS = 128
CONV_HALO = 16
LANES = 128
SUBLANES = 8
ROW_CHUNK = 16
VMEM_LIMIT = 56 * 1024 * 1024


def _params(sem):
    return pltpu.CompilerParams(dimension_semantics=sem, vmem_limit_bytes=VMEM_LIMIT)


def _rms(t):
    return t * lax.rsqrt(jnp.mean(t * t, axis=-1, keepdims=True) + NORM_EPS)


def _modulated(x, gain, shift, scale):
    return (_rms(x) * (gain * (1.0 + scale)) + shift).astype(BF16)


def _pack_pair(lo, hi):
    lo_bits = lax.bitcast_convert_type(lo.astype(BF16).astype(F32), U32)
    hi_bits = lax.bitcast_convert_type(hi.astype(BF16).astype(F32), U32)
    return (lo_bits >> 16) | hi_bits


def _unpack_pair(w):
    lo = lax.bitcast_convert_type(w << 16, F32).astype(BF16)
    hi = lax.bitcast_convert_type(w & jnp.uint32(0xFFFF0000), F32).astype(BF16)
    return lo, hi


def _modulate_rows(src_ref, dst_ref, dst_row0, gain, shift, scale):
    rows = src_ref.shape[0]
    eff = gain * (1.0 + scale)

    def scoped(rstd_ref):
        _row_rstd(src_ref, rstd_ref)

        def body(r, carry):
            sl = _chunk(r)
            dst_ref[pl.ds(dst_row0 + sl.start, ROW_CHUNK), :] = (
                src_ref[sl, :] * _lanes(rstd_ref[sl, :], src_ref.shape[1]) * eff + shift).astype(BF16)
            return carry

        lax.fori_loop(0, rows // ROW_CHUNK, body, 0, unroll=2)

    pl.run_scoped(scoped, pltpu.VMEM((rows, LANES), F32))


def _gated_residual_rows(y_ref, x_ref, o_ref, gate, gain):
    rows = y_ref.shape[0]
    eff = gate * gain

    def scoped(rstd_ref):
        _row_rstd(y_ref, rstd_ref)

        def body(r, carry):
            sl = _chunk(r)
            o_ref[sl, :] = x_ref[sl, :] + y_ref[sl, :] * _lanes(rstd_ref[sl, :], y_ref.shape[1]) * eff
            return carry

        lax.fori_loop(0, rows // ROW_CHUNK, body, 0, unroll=2)

    pl.run_scoped(scoped, pltpu.VMEM((rows, LANES), F32))


def _lanes(t, width):
    return jnp.concatenate([t] * (width // LANES), axis=1)


def _chunk(r):
    return pl.ds(pl.multiple_of(r * ROW_CHUNK, ROW_CHUNK), ROW_CHUNK)


def _row_rstd(src_ref, rstd_ref):
    def body(r, carry):
        sl = _chunk(r)
        t = src_ref[sl, :]
        rstd = lax.rsqrt(jnp.mean(t * t, axis=-1, keepdims=True) + NORM_EPS)
        rstd_ref[sl, :] = jnp.broadcast_to(rstd, (ROW_CHUNK, LANES))
        return carry

    lax.fori_loop(0, src_ref.shape[0] // ROW_CHUNK, body, 0, unroll=True)


def _mod_kernel(c_ref, w_ref, b_ref, o_ref):
    cf = c_ref[...]
    s = (cf * jax.nn.sigmoid(cf)).astype(BF16)
    o_ref[0] = jnp.dot(s, w_ref[0].astype(BF16), preferred_element_type=F32) + b_ref[0]


def _modulation(cc, w_mod, b_mod):
    depth, d, n = w_mod.shape
    tn = 1024
    return pl.pallas_call(
        _mod_kernel,
        grid=(depth, n // tn),
        in_specs=[pl.BlockSpec((MOD_ROWS, d), lambda l, j: (0, 0)),
                  pl.BlockSpec((1, d, tn), lambda l, j: (l, 0, j)),
                  pl.BlockSpec((1, 1, tn), lambda l, j: (l, 0, j))],
        out_specs=pl.BlockSpec((1, MOD_ROWS, tn), lambda l, j: (l, 0, j)),
        out_shape=jax.ShapeDtypeStruct((depth, MOD_ROWS, n), F32),
        compiler_params=_params(("arbitrary", "arbitrary")),
        name="modulation",
    )(cc, w_mod, b_mod.reshape(depth, 1, n))


def _even_in_kernel(x_ref, mod_ref, g_ref, w_ref, cs_ref, cos_ref, sin_ref,
                    u_ref, q_ref, k_ref, v_ref):
    tm = x_ref.shape[1]
    u = _modulated(x_ref[0], g_ref[0:1, :], mod_ref[0, 0:1, :], mod_ref[0, 1:2, :])
    p = jnp.dot(u, w_ref[...], preferred_element_type=F32)
    cs = cs_ref[...]
    gd = FOURIER_GROUP_DIM
    for g in range(N_FOURIER_GROUPS):
        t = jnp.dot(p[:, g * gd:(g + 1) * gd].astype(BF16), cs, preferred_element_type=F32)
        u_ref[0, :, g * gd:(g + 1) * gd] = _pack_pair(t[:, :gd], t[:, gd:])
    cos = cos_ref[...]
    sin = sin_ref[...]
    lane = lax.broadcasted_iota(jnp.int32, (tm, HEAD_DIM), 1)
    first_half = (lane % ROPE_AXIS_DIM) < (ROPE_AXIS_DIM // 2)

    def rope(h):
        partner = jnp.where(first_half, pltpu.roll(h, HEAD_DIM - 32, 1), pltpu.roll(h, 32, 1))
        return h * cos + partner * sin

    q0 = FOURIER_WIDTH
    for h in range(N_HEADS):
        hq = p[:, q0 + h * HEAD_DIM:q0 + (h + 1) * HEAD_DIM]
        q_ref[0, :, h * HEAD_DIM:(h + 1) * HEAD_DIM] = (rope(hq) * (ATTN_SCALE * LOG2_E)).astype(BF16)
    k0 = FOURIER_WIDTH + Q_WIDTH
    for h in range(N_KV_HEADS):
        hk = p[:, k0 + h * HEAD_DIM:k0 + (h + 1) * HEAD_DIM]
        k_ref[0, :, h * HEAD_DIM:(h + 1) * HEAD_DIM] = rope(hk).astype(BF16)
    v_ref[0] = p[:, k0 + KV_WIDTH:].astype(BF16)


def _even_in(x, mod, gains, w_in, li, cs, cos, sin, tm):
    b, n, d = x.shape
    tok = lambda bi, i: (bi, i, 0)
    const = lambda bi, i: (0, 0)
    return pl.pallas_call(
        _even_in_kernel,
        grid=(b, n // tm),
        in_specs=[pl.BlockSpec((1, tm, d), tok),
                  pl.BlockSpec((1, MOD_ROWS, d), lambda bi, i: (bi, 0, 0)),
                  pl.BlockSpec((MOD_ROWS, d), const),
                  pl.BlockSpec((None, d, EVEN_IN_WIDTH), lambda bi, i: (li, 0, 0)),
                  pl.BlockSpec((FOURIER_GROUP_DIM, 2 * FOURIER_GROUP_DIM), const),
                  pl.BlockSpec((tm, HEAD_DIM), lambda bi, i: (i, 0)),
                  pl.BlockSpec((tm, HEAD_DIM), lambda bi, i: (i, 0))],
        out_specs=[pl.BlockSpec((1, tm, FOURIER_WIDTH), tok),
                   pl.BlockSpec((1, tm, Q_WIDTH), tok),
                   pl.BlockSpec((1, tm, KV_WIDTH), tok),
                   pl.BlockSpec((1, tm, KV_WIDTH), tok)],
        out_shape=[jax.ShapeDtypeStruct((b, n, FOURIER_WIDTH), U32),
                   jax.ShapeDtypeStruct((b, n, Q_WIDTH), BF16),
                   jax.ShapeDtypeStruct((b, n, KV_WIDTH), BF16),
                   jax.ShapeDtypeStruct((b, n, KV_WIDTH), BF16)],
        compiler_params=_params(("arbitrary", "arbitrary")),
        name="even_in_proj",
    )(x, mod, gains, w_in, cs, cos, sin)


def _dft1_kernel(u_ref, t_ref, o_ref, w_sc):
    na = u_ref.shape[1]
    for j in range(SUBLANES):
        w_sc[...] = u_ref[0, :, j, :]
        uc, us = _unpack_pair(w_sc[...])
        res = jnp.dot(t_ref[j], jnp.concatenate([uc, us], axis=0), preferred_element_type=F32)
        o_ref[0, j] = _pack_pair(res[:na], res[na:])


def _dft1(u4, tab):
    b, na, nb, w = u4.shape
    return pl.pallas_call(
        _dft1_kernel,
        grid=(nb // SUBLANES, b),
        in_specs=[pl.BlockSpec((1, na, SUBLANES, w), lambda j, bi: (bi, 0, j, 0)),
                  pl.BlockSpec((SUBLANES, 2 * na, 2 * na), lambda j, bi: (j, 0, 0))],
        out_specs=pl.BlockSpec((1, SUBLANES, na, w), lambda j, bi: (bi, j, 0, 0)),
        out_shape=jax.ShapeDtypeStruct((b, nb, na, w), U32),
        scratch_shapes=[pltpu.VMEM((na, w), U32)],
        compiler_params=_params(("arbitrary", "arbitrary")),
        name="seq_dft_stage1",
    )(u4, tab)


def _dft2_kernel(a_ref, m_ref, o_ref, w_sc):
    half = a_ref.shape[3] // 2
    for j in range(SUBLANES):
        w_sc[...] = a_ref[0, :, j, :]
        ar, ai = _unpack_pair(w_sc[...])
        y = jnp.dot(m_ref[...], jnp.concatenate([ar, ai], axis=0), preferred_element_type=F32)
        o_ref[0, j] = _pack_pair(y[:, :half], y[:, half:])


def _dft2(a4, m2):
    b, nb, na, w = a4.shape
    return pl.pallas_call(
        _dft2_kernel,
        grid=(b, na // SUBLANES),
        in_specs=[pl.BlockSpec((1, nb, SUBLANES, w), lambda bi, j: (bi, 0, j, 0)),
                  pl.BlockSpec((nb, 2 * nb), lambda bi, j: (0, 0))],
        out_specs=pl.BlockSpec((1, SUBLANES, nb, w // 2), lambda bi, j: (bi, j, 0, 0)),
        out_shape=jax.ShapeDtypeStruct((b, na, nb, w // 2), U32),
        scratch_shapes=[pltpu.VMEM((nb, w), U32)],
        compiler_params=_params(("arbitrary", "arbitrary")),
        name="seq_dft_stage2",
    )(a4, m2)


def _seq_dft_tables(n):
    nb = DFT_COLS
    na = n // nb
    a = np.arange(na)
    dd = np.arange(na)
    bb = np.arange(nb)
    ph = (dd[None, :, None] * (nb * a[None, None, :] + bb[:, None, None])) % n
    ang = 2.0 * np.pi * ph / n
    c1, s1 = np.cos(ang), np.sin(ang)
    tab1 = np.concatenate([np.concatenate([c1, -s1], axis=2),
                           np.concatenate([s1, c1], axis=2)], axis=1)
    cidx = np.arange(nb)
    ang2 = 2.0 * np.pi * ((cidx[:, None] * bb[None, :]) % nb) / nb
    m2 = np.concatenate([np.cos(ang2), -np.sin(ang2)], axis=1) * n ** -0.5
    return jnp.asarray(tab1, BF16), jnp.asarray(m2, BF16)


def _channel_dft_table():
    i = np.arange(FOURIER_GROUP_DIM)
    ang = 2.0 * np.pi * ((i[:, None] * i[None, :]) % FOURIER_GROUP_DIM) / FOURIER_GROUP_DIM
    cs = np.concatenate([np.cos(ang), np.sin(ang)], axis=1) * FOURIER_GROUP_DIM ** -0.5
    return jnp.asarray(cs, BF16)


def _fourier_seq(u):
    b, n, w = u.shape
    nb = DFT_COLS
    na = n // nb
    tab1, m2 = _seq_dft_tables(n)
    return _dft2(_dft1(u.reshape(b, na, nb, w), tab1), m2)


def _ctx_dft_kernel(u_ref, m_ref, o_ref):
    half = u_ref.shape[2] // 2
    uc, us = _unpack_pair(u_ref[0])
    y = jnp.dot(m_ref[...], jnp.concatenate([uc, us], axis=0), preferred_element_type=F32)
    o_ref[0] = _pack_pair(y[:, :half], y[:, half:])


def _fourier_ctx(u):
    b, n, w = u.shape
    i = np.arange(n)
    ang = 2.0 * np.pi * ((i[:, None] * i[None, :]) % n) / n
    m = jnp.asarray(np.concatenate([np.cos(ang), -np.sin(ang)], axis=1) * n ** -0.5, BF16)
    return pl.pallas_call(
        _ctx_dft_kernel,
        grid=(b,),
        in_specs=[pl.BlockSpec((1, n, w), lambda bi: (bi, 0, 0)),
                  pl.BlockSpec((n, 2 * n), lambda bi: (0, 0))],
        out_specs=pl.BlockSpec((1, n, w // 2), lambda bi: (bi, 0, 0)),
        out_shape=jax.ShapeDtypeStruct((b, n, w // 2), U32),
        compiler_params=_params(("arbitrary",)),
        name="ctx_dft",
    )(u, m)


def _sink_column(sink_ref, kvh, rows):
    parts = [jnp.full((rows, 1), sink_ref[kvh * GQA_GROUP + g] * LOG2_E, F32) for g in range(GQA_GROUP)]
    return jnp.concatenate(parts, axis=0)


def _softmax_pv(s, sink_col, v):
    m = jnp.maximum(sink_col, jnp.max(s, axis=-1, keepdims=True))
    e = jnp.exp2(s - m)
    den = jnp.exp2(sink_col - m) + jnp.sum(e, axis=-1, keepdims=True)
    return jnp.dot(e.astype(BF16), v, preferred_element_type=F32) / den


def _window_attn_kernel(sink_ref, q_ref, km_ref, kp_ref, kn_ref, vm_ref, vp_ref, vn_ref,
                        kc_ref, vc_ref, o_ref):
    i = pl.program_id(1)
    kvh = pl.program_id(2)
    n_tiles = pl.num_programs(1)
    blocks = q_ref.shape[1] // BLOCK
    n_ctx = kc_ref.shape[1]
    n_keys = 3 * BLOCK + n_ctx
    n_q = GQA_GROUP * BLOCK
    sink_row = jnp.concatenate([jnp.full((1, BLOCK), sink_ref[kvh * GQA_GROUP + g] * LOG2_E, F32)
                                for g in range(GQA_GROUP)], axis=1)
    kj = lax.broadcasted_iota(jnp.int32, (n_keys, n_q), 0)
    qi = lax.broadcasted_iota(jnp.int32, (n_keys, n_q), 1) % BLOCK

    def band(has_prev, has_next):
        lo = qi + jnp.where(has_prev, 0, BLOCK)
        hi = qi + jnp.where(has_next, BLOCK, 0)
        return ((kj >= lo) & (kj < BLOCK)) | ((kj >= BLOCK) & (kj <= hi)) | (kj >= 2 * BLOCK)

    for jb in range(blocks):
        sl = slice(jb * BLOCK, (jb + 1) * BLOCK)
        q4 = jnp.concatenate([q_ref[0, sl, g * HEAD_DIM:(g + 1) * HEAD_DIM] for g in range(GQA_GROUP)],
                             axis=0)
        if jb == 0:
            k_prev, v_prev, has_prev = kp_ref[0], vp_ref[0], i > 0
        else:
            psl = slice((jb - 1) * BLOCK, jb * BLOCK)
            k_prev, v_prev, has_prev = km_ref[0, psl], vm_ref[0, psl], True
        if jb == blocks - 1:
            k_next, v_next, has_next = kn_ref[0], vn_ref[0], i < n_tiles - 1
        else:
            nsl = slice((jb + 1) * BLOCK, (jb + 2) * BLOCK)
            k_next, v_next, has_next = km_ref[0, nsl], vm_ref[0, nsl], True
        keys = jnp.concatenate([k_prev, k_next, km_ref[0, sl], kc_ref[0]], axis=0)
        vals = jnp.concatenate([v_prev, v_next, vm_ref[0, sl], vc_ref[0]], axis=0)
        st = lax.dot_general(keys, q4, (((1,), (1,)), ((), ())), preferred_element_type=F32)
        st = jnp.where(band(has_prev, has_next), st, NEG_INF)
        m = jnp.maximum(sink_row, jnp.max(st, axis=0, keepdims=True))
        e = jnp.exp2(st - m)
        den = jnp.exp2(sink_row - m) + jnp.sum(e, axis=0, keepdims=True)
        ot = lax.dot_general(vals, e.astype(BF16), (((0,), (0,)), ((), ())), preferred_element_type=F32)
        ot = ot / den
        for g in range(GQA_GROUP):
            o_ref[0, sl, g * HEAD_DIM:(g + 1) * HEAD_DIM] = ot[:, g * BLOCK:(g + 1) * BLOCK].T.astype(BF16)


def _window_attention(q, k, v, kc, vc, sink, tq):
    b, n, _ = q.shape
    n_ctx = kc.shape[1]
    bpt = tq // BLOCK
    nblk = n // BLOCK
    main = lambda bi, i, h: (bi, i, h)
    prev = lambda bi, i, h: (bi, jnp.maximum(i * bpt - 1, 0), h)
    nxt = lambda bi, i, h: (bi, jnp.minimum((i + 1) * bpt, nblk - 1), h)
    ctx = lambda bi, i, h: (bi, 0, h)
    kv_main = pl.BlockSpec((1, tq, HEAD_DIM), main)
    kv_prev = pl.BlockSpec((1, BLOCK, HEAD_DIM), prev)
    kv_next = pl.BlockSpec((1, BLOCK, HEAD_DIM), nxt)
    kv_ctx = pl.BlockSpec((1, n_ctx, HEAD_DIM), ctx)
    return pl.pallas_call(
        _window_attn_kernel,
        grid=(b, n // tq, N_KV_HEADS),
        in_specs=[pl.BlockSpec(memory_space=pltpu.SMEM),
                  pl.BlockSpec((1, tq, GQA_GROUP * HEAD_DIM), main),
                  kv_main, kv_prev, kv_next, kv_main, kv_prev, kv_next, kv_ctx, kv_ctx],
        out_specs=pl.BlockSpec((1, tq, GQA_GROUP * HEAD_DIM), main),
        out_shape=jax.ShapeDtypeStruct((b, n, Q_WIDTH), BF16),
        compiler_params=_params(("arbitrary", "arbitrary", "arbitrary")),
        name="window_attention",
    )(sink, q, k, k, k, v, v, v, kc, vc)


def _ctx_attn_kernel(sink_ref, q_ref, k_ref, v_ref, o_ref):
    kvh = pl.program_id(1)
    n = q_ref.shape[1]
    q4 = jnp.concatenate([q_ref[0, :, g * HEAD_DIM:(g + 1) * HEAD_DIM] for g in range(GQA_GROUP)], axis=0)
    s = lax.dot_general(q4, k_ref[0], (((1,), (1,)), ((), ())), preferred_element_type=F32)
    o = _softmax_pv(s, _sink_column(sink_ref, kvh, n), v_ref[0])
    for g in range(GQA_GROUP):
        o_ref[0, :, g * HEAD_DIM:(g + 1) * HEAD_DIM] = o[g * n:(g + 1) * n].astype(BF16)


def _ctx_attention(q, k, v, sink):
    b, n, _ = q.shape
    hq = pl.BlockSpec((1, n, GQA_GROUP * HEAD_DIM), lambda bi, h: (bi, 0, h))
    hkv = pl.BlockSpec((1, n, HEAD_DIM), lambda bi, h: (bi, 0, h))
    return pl.pallas_call(
        _ctx_attn_kernel,
        grid=(b, N_KV_HEADS),
        in_specs=[pl.BlockSpec(memory_space=pltpu.SMEM), hq, hkv, hkv],
        out_specs=hq,
        out_shape=jax.ShapeDtypeStruct((b, n, Q_WIDTH), BF16),
        compiler_params=_params(("arbitrary", "arbitrary")),
        name="ctx_attention",
    )(sink, q, k, v)


def _even_out_kernel(f_ref, a_ref, w_ref, x_ref, mod_ref, g_ref, o_ref, w_sc, *, cols_per_tile):
    half = FOURIER_WIDTH // 2
    if cols_per_tile:
        na = f_ref.shape[1]
        c0 = (pl.program_id(1) * cols_per_tile) % SUBLANES
        for jc in range(cols_per_tile):
            w_sc[jc * na:(jc + 1) * na, :] = f_ref[0, :, c0 + jc, :]
    else:
        w_sc[...] = f_ref[0]
    f_lo, f_hi = _unpack_pair(w_sc[...])
    mix = jnp.dot(f_lo, w_ref[:half, :], preferred_element_type=F32)
    mix = mix + jnp.dot(f_hi, w_ref[half:FOURIER_WIDTH, :], preferred_element_type=F32)
    mix = mix + jnp.dot(a_ref[0], w_ref[FOURIER_WIDTH:, :], preferred_element_type=F32)
    o_ref[0] = x_ref[0] + mod_ref[0, 2:3, :] * (_rms(mix) * g_ref[1:2, :])


def _even_out(four, attn, w_out, li, x, mod, gains, tm):
    b, n, d = x.shape
    tok = lambda bi, i: (bi, i, 0)
    const = lambda bi, i: (0, 0)
    half = FOURIER_WIDTH // 2
    if four.ndim == 4:
        na = four.shape[1]
        cols_per_tile = tm // na
        assert tm == cols_per_tile * na and SUBLANES % cols_per_tile == 0
        four_spec = pl.BlockSpec((1, na, SUBLANES, half),
                                 lambda bi, i: (bi, 0, (i * cols_per_tile) // SUBLANES, 0))
    else:
        cols_per_tile = 0
        four_spec = pl.BlockSpec((1, tm, half), tok)
    return pl.pallas_call(
        functools.partial(_even_out_kernel, cols_per_tile=cols_per_tile),
        grid=(b, n // tm),
        in_specs=[four_spec,
                  pl.BlockSpec((1, tm, Q_WIDTH), tok),
                  pl.BlockSpec((None, FOURIER_WIDTH + Q_WIDTH, d), lambda bi, i: (li, 0, 0)),
                  pl.BlockSpec((1, tm, d), tok),
                  pl.BlockSpec((1, MOD_ROWS, d), lambda bi, i: (bi, 0, 0)),
                  pl.BlockSpec((MOD_ROWS, d), const)],
        out_specs=pl.BlockSpec((1, tm, d), tok),
        out_shape=jax.ShapeDtypeStruct((b, n, d), F32),
        scratch_shapes=[pltpu.VMEM((tm, half), U32)],
        compiler_params=_params(("arbitrary", "arbitrary")),
        name="even_out_proj",
    )(four, attn, w_out, x, mod, gains)


def _conv_mix_kernel(x_ref, xp_ref, xn_ref, mod_ref, g_ref, wb_ref, wc_ref, wx_ref, cw_ref, wo_ref,
                     o_ref, u_sc, z_sc, acc_sc):
    i = pl.program_id(1)
    j = pl.program_id(2)
    tm = x_ref.shape[1]
    h = CONV_HALO

    @pl.when(j == 0)
    def _():
        gain, shift, scale = g_ref[0:1, :], mod_ref[0, 0:1, :], mod_ref[0, 1:2, :]
        keep_prev = jnp.where(i > 0, 1.0, 0.0).astype(BF16)
        keep_next = jnp.where(i < pl.num_programs(1) - 1, 1.0, 0.0).astype(BF16)
        u_sc[0:h, :] = _modulated(xp_ref[0], gain, shift, scale) * keep_prev
        _modulate_rows(x_ref.at[0], u_sc, h, gain, shift, scale)
        u_sc[h + tm:, :] = _modulated(xn_ref[0], gain, shift, scale) * keep_next
        acc_sc[...] = jnp.zeros_like(acc_sc)

    u_all = u_sc[...]
    gb = jnp.dot(u_sc[h:h + tm, :], wb_ref[...], preferred_element_type=F32)
    z_sc[...] = (jnp.dot(u_all, wc_ref[...], preferred_element_type=F32)
                 * jnp.dot(u_all, wx_ref[...], preferred_element_type=F32))
    y = (cw_ref[0:1, :] * z_sc[h - 1:h - 1 + tm, :]
         + cw_ref[1:2, :] * z_sc[h:h + tm, :]
         + cw_ref[2:3, :] * z_sc[h + 1:h + 1 + tm, :])
    acc_sc[...] += jnp.dot((gb * y).astype(BF16), wo_ref[...], preferred_element_type=F32)

    @pl.when(j == pl.num_programs(2) - 1)
    def _():
        _gated_residual_rows(acc_sc, x_ref.at[0], o_ref.at[0], mod_ref[0, 2:3, :], g_ref[1:2, :])


def _conv_mix(x, mod, gains, w_in, conv_w, w_out, li, tm, tc):
    b, n, d = x.shape
    h = CONV_HALO
    nch = d // tc
    hb = tm // h
    last = n // h - 1
    tok = lambda bi, i, j: (bi, i, 0)
    return pl.pallas_call(
        _conv_mix_kernel,
        grid=(b, n // tm, nch),
        in_specs=[pl.BlockSpec((1, tm, d), tok),
                  pl.BlockSpec((1, h, d), lambda bi, i, j: (bi, jnp.maximum(i * hb - 1, 0), 0)),
                  pl.BlockSpec((1, h, d), lambda bi, i, j: (bi, jnp.minimum((i + 1) * hb, last), 0)),
                  pl.BlockSpec((1, MOD_ROWS, d), lambda bi, i, j: (bi, 0, 0)),
                  pl.BlockSpec((MOD_ROWS, d), lambda bi, i, j: (0, 0)),
                  pl.BlockSpec((None, d, tc), lambda bi, i, j: (li, 0, j)),
                  pl.BlockSpec((None, d, tc), lambda bi, i, j: (li, 0, nch + j)),
                  pl.BlockSpec((None, d, tc), lambda bi, i, j: (li, 0, 2 * nch + j)),
                  pl.BlockSpec((MOD_ROWS, tc), lambda bi, i, j: (0, j)),
                  pl.BlockSpec((None, tc, d), lambda bi, i, j: (li, j, 0))],
        out_specs=pl.BlockSpec((1, tm, d), tok),
        out_shape=jax.ShapeDtypeStruct((b, n, d), F32),
        scratch_shapes=[pltpu.VMEM((tm + 2 * h, d), BF16),
                        pltpu.VMEM((tm + 2 * h, tc), F32),
                        pltpu.VMEM((tm, d), F32)],
        compiler_params=_params(("arbitrary", "arbitrary", "arbitrary")),
        name="conv_mixer",
    )(x, x, x, mod, gains, w_in, w_in, w_in, conv_w, w_out)


def _mlp_kernel(x_ref, mod_ref, g_ref, w1_ref, w2_ref, o_ref, u_sc, acc_sc):
    f = pl.program_id(2)

    @pl.when(f == 0)
    def _():
        _modulate_rows(x_ref.at[0], u_sc, 0, g_ref[2:3, :], mod_ref[0, 3:4, :], mod_ref[0, 4:5, :])
        acc_sc[...] = jnp.zeros_like(acc_sc)

    hid = jnp.maximum(jnp.dot(u_sc[...], w1_ref[...], preferred_element_type=F32), 0.0)
    acc_sc[...] += jnp.dot((hid * hid).astype(BF16), w2_ref[...], preferred_element_type=F32)

    @pl.when(f == pl.num_programs(2) - 1)
    def _():
        _gated_residual_rows(acc_sc, x_ref.at[0], o_ref.at[0], mod_ref[0, 5:6, :], g_ref[3:4, :])


def _mlp(x, mod, gains, w1, w2, li, tm, tf):
    b, n, d = x.shape
    dff = w1.shape[2]
    tok = lambda bi, i, f: (bi, i, 0)
    return pl.pallas_call(
        _mlp_kernel,
        grid=(b, n // tm, dff // tf),
        in_specs=[pl.BlockSpec((1, tm, d), tok),
                  pl.BlockSpec((1, MOD_ROWS, d), lambda bi, i, f: (bi, 0, 0)),
                  pl.BlockSpec((MOD_ROWS, d), lambda bi, i, f: (0, 0)),
                  pl.BlockSpec((None, d, tf), lambda bi, i, f: (li, 0, f)),
                  pl.BlockSpec((None, tf, d), lambda bi, i, f: (li, f, 0))],
        out_specs=pl.BlockSpec((1, tm, d), tok),
        out_shape=jax.ShapeDtypeStruct((b, n, d), F32),
        scratch_shapes=[pltpu.VMEM((tm, d), BF16), pltpu.VMEM((tm, d), F32)],
        compiler_params=_params(("arbitrary", "arbitrary", "arbitrary")),
        name="sq_relu_mlp",
    )(x, mod, gains, w1, w2)


def _rope_tables(n):
    rows = n // GRID_W
    row = np.repeat(np.arange(rows, dtype=np.float32), GRID_W)
    col = np.tile(np.arange(GRID_W, dtype=np.float32), rows)
    expo = -np.arange(0, ROPE_AXIS_DIM, 2, dtype=np.float32) / np.float32(ROPE_AXIS_DIM)
    inv_freq = np.power(np.float32(ROPE_THETA), expo).astype(np.float32)
    ar, ac = row[:, None] * inv_freq, col[:, None] * inv_freq
    cos = np.concatenate([np.cos(ar), np.cos(ar), np.cos(ac), np.cos(ac)], axis=1)
    sin = np.concatenate([-np.sin(ar), np.sin(ar), -np.sin(ac), np.sin(ac)], axis=1)
    return jnp.asarray(cos, F32), jnp.asarray(sin, F32)


def _pad_rows(t):
    return jnp.pad(t, ((0, MOD_ROWS - t.shape[0]), (0, 0)))


def _tile(n, want):
    return min(n, want)


def kernel(x, c, ctx, c_ctx, w_mod, b_mod, norm_gains, att_w_in, att_sink, att_w_out,
           conv_w_in, conv_w, conv_w_out, mlp_w1, mlp_w2):
    bsz, n_tok, d = x.shape
    n_ctx = ctx.shape[1]
    depth = w_mod.shape[0]

    cc = _pad_rows(jnp.concatenate([c, c_ctx[None, :]], axis=0))
    mod_all = _modulation(cc, w_mod, b_mod)
    cos_x, sin_x = _rope_tables(n_tok)
    cos_c, sin_c = jnp.ones((n_ctx, HEAD_DIM), F32), jnp.zeros((n_ctx, HEAD_DIM), F32)
    cs = _channel_dft_table()
    att_in, att_out = att_w_in.astype(BF16), att_w_out.astype(BF16)
    conv_in, conv_out = conv_w_in.astype(BF16), conv_w_out.astype(BF16)
    w1, w2 = mlp_w1.astype(BF16), mlp_w2.astype(BF16)

    tm_x = _tile(n_tok, 512)
    h_ctx = ctx
    for l in range(depth):
        gains = _pad_rows(norm_gains[l])
        mod_x = jnp.stack([_pad_rows(mod_all[l, bi].reshape(N_MOD, d)) for bi in range(bsz)])
        mod_c1 = _pad_rows(mod_all[l, bsz].reshape(N_MOD, d))[None]
        mod_c = jnp.broadcast_to(mod_c1, (bsz, MOD_ROWS, d))
        ctx_update = any(j % 2 == 0 for j in range(l + 1, depth))
        if l % 2 == 0:
            a = l // 2
            sink = att_sink[a]
            u_c, q_c, k_c, v_c = _even_in(h_ctx, mod_c, gains, att_in, a, cs, cos_c, sin_c, n_ctx)
            u_x, q_x, k_x, v_x = _even_in(x, mod_x, gains, att_in, a, cs, cos_x, sin_x, tm_x)
            attn_x = _window_attention(q_x, k_x, v_x, k_c, v_c, sink, _tile(n_tok, 1024))
            x = _even_out(_fourier_seq(u_x), attn_x, att_out, a, x, mod_x, gains, 4 * (n_tok // DFT_COLS))
            if ctx_update:
                attn_c = _ctx_attention(q_c, k_c, v_c, sink)
                h_ctx = _even_out(_fourier_ctx(u_c), attn_c, att_out, a, h_ctx, mod_c, gains, n_ctx)
        else:
            m = l // 2
            cw = _pad_rows(conv_w[m])
            x = _conv_mix(x, mod_x, gains, conv_in, cw, conv_out, m, tm_x, 512)
            if ctx_update:
                h_ctx = _conv_mix(h_ctx, mod_c, gains, conv_in, cw, conv_out, m, n_ctx, 512)
        x = _mlp(x, mod_x, gains, w1, w2, l, tm_x, 1024)
        if ctx_update:
            flat = _mlp(h_ctx.reshape(1, bsz * n_ctx, d), mod_c1, gains, w1, w2, l, bsz * n_ctx, 1024)
            h_ctx = flat.reshape(bsz, n_ctx, d)
    return x
```
